```python
import jax, jax.numpy as jnp
from jax import lax
import numpy as np

D_MODEL = 1024
BATCH = 2
SEQ = 8192
DEPTH = 4
DEC_BATCH = 32
DEC_SEQ = 4
PAST_LEN = 8192
PAGE_SIZE = 128

HEAD_DIM = 64
MIX_WIDTH = D_MODEL
H_NSA = MIX_WIDTH // 2 // HEAD_DIM
H_FOX = MIX_WIDTH // 2 // HEAD_DIM
G_NSA = 2
HPG = H_NSA // G_NSA
NSA_BLOCK = 64
NSA_TOPK = 16
NSA_WINDOW = 512
Q_BLOCK = 128
D_FF = ((8 * D_MODEL // 3 + 127) // 128) * 128
CONV_W = 3
RMS_EPS = 1e-6
NEG_INF = -1e30
FORCE_SCORE = 1e4
FORGET_BIAS_MEAN = 2.0

W_NSA_Q = H_NSA * HEAD_DIM
W_NSA_KV = 6 * G_NSA * HEAD_DIM
W_NSA_GATE = 3 * H_NSA
W_FOX_Q = H_FOX * HEAD_DIM
W_FOX_KV = 2 * H_FOX * HEAD_DIM
W_FOX_F = H_FOX
MIX_IN = W_NSA_Q + W_NSA_KV + W_NSA_GATE + W_FOX_Q + W_FOX_KV + W_FOX_F
SPLITS = (W_NSA_Q,
          W_NSA_Q + W_NSA_KV,
          W_NSA_Q + W_NSA_KV + W_NSA_GATE,
          W_NSA_Q + W_NSA_KV + W_NSA_GATE + W_FOX_Q,
          W_NSA_Q + W_NSA_KV + W_NSA_GATE + W_FOX_Q + W_FOX_KV)
SCALE = HEAD_DIM ** -0.5

kernel_name = "nsa_fox_hybrid_decoder_step"


def rms_norm(x, g):
    xf = x.astype(jnp.float32)
    y = xf * lax.rsqrt(jnp.mean(xf * xf, axis=-1, keepdims=True) + RMS_EPS)
    return (y * g.astype(jnp.float32)).astype(x.dtype)


def masked_softmax(s, mask, axis):
    s = jnp.where(mask, s.astype(jnp.float32), NEG_INF)
    m = jnp.max(s, axis=axis, keepdims=True)
    e = jnp.where(mask, jnp.exp(s - m), 0.0)
    d = jnp.sum(e, axis=axis, keepdims=True)
    return e / jnp.maximum(d, 1e-30)


def alibi_slopes():
    h = jnp.arange(1, H_NSA + 1, dtype=jnp.float32)
    return (2.0 ** (-8.0 * h / H_NSA)).reshape(G_NSA, HPG)


def project(h, w_in, b_gate, b_forget):
    B, T, _ = h.shape
    u = h @ w_in
    q_n, kv_n, gate, q_f, kv_f, f = jnp.split(u, SPLITS, axis=-1)
    q_n = q_n.reshape(B, T, G_NSA, HPG, HEAD_DIM)
    kv_n = kv_n.reshape(B, T, 6, G_NSA, HEAD_DIM)
    gate = jax.nn.sigmoid(gate + b_gate).reshape(B, T, 3, G_NSA, HPG)
    q_f = q_f.reshape(B, T, H_FOX, HEAD_DIM)
    kv_f = kv_f.reshape(B, T, 2, H_FOX, HEAD_DIM)
    logf = jax.nn.log_sigmoid((f + b_forget).astype(jnp.float32))
    return q_n, kv_n, gate, q_f, kv_f, logf


def compress(k, v, pe, wk, wv):
    B, L = k.shape[:2]
    n = L // NSA_BLOCK
    kb = k[:, :n * NSA_BLOCK].reshape(B, n, NSA_BLOCK, G_NSA, HEAD_DIM) + pe[None, None, :, None, :]
    vb = v[:, :n * NSA_BLOCK].reshape(B, n, NSA_BLOCK, G_NSA, HEAD_DIM) + pe[None, None, :, None, :]
    kc = jnp.einsum('bnlgd,lde->bnge', kb, wk)
    vc = jnp.einsum('bnlgd,lde->bnge', vb, wv)
    return kc, vc


def to_blocks(x, ns):
    B, L = x.shape[:2]
    x = jnp.pad(x, ((0, 0), (0, ns * NSA_BLOCK - L), (0, 0), (0, 0)))
    return x.reshape(B, ns, NSA_BLOCK, G_NSA, HEAD_DIM).transpose(0, 3, 1, 2, 4)


def gather_blocks(blocks, idx):
    return jax.vmap(jax.vmap(lambda bl, ix: bl[ix]))(blocks, idx)


def nsa_sparse(q, qpos, kc, vc, ksb, vsb, slopes):
    nc = kc.shape[1]
    ns = ksb.shape[2]
    s = jnp.einsum('btghd,bcgd->bghtc', q, kc).astype(jnp.float32) * SCALE
    cend = jnp.arange(nc) * NSA_BLOCK + (NSA_BLOCK - 1)
    dist = qpos[:, None] - cend[None, :]
    s = s - slopes[None, :, :, None, None] * dist.astype(jnp.float32)
    p = masked_softmax(s, dist >= 0, -1)
    o_cmp = jnp.einsum('bghtc,bcgd->btghd', p.astype(vc.dtype), vc)
    imp = jnp.pad(p.sum(axis=2), ((0, 0), (0, 0), (0, 0), (0, ns - nc)))
    cur = (qpos // NSA_BLOCK)[:, None]
    j = jnp.arange(ns)[None, :]
    forced = (j == 0) | (j == cur) | (j == cur - 1)
    score = jnp.where(j <= cur, jnp.where(forced, FORCE_SCORE, imp), -1.0)
    top_s, idx = lax.top_k(score, min(NSA_TOPK, ns))
    kg = gather_blocks(ksb, idx)
    vg = gather_blocks(vsb, idx)
    spos = idx[..., None] * NSA_BLOCK + jnp.arange(NSA_BLOCK)
    dist2 = qpos[None, None, :, None, None] - spos
    ok = (top_s >= 0)[..., None] & (dist2 >= 0)
    s2 = (jnp.einsum('btghd,bgtkld->bghtkl', q, kg).astype(jnp.float32) * SCALE
          - slopes[None, :, :, None, None, None] * dist2[:, :, None].astype(jnp.float32))
    p2 = masked_softmax(s2, ok[:, :, None], (-2, -1))
    o_slc = jnp.einsum('bghtkl,bgtkld->btghd', p2.astype(vg.dtype), vg)
    return o_cmp, o_slc


def window_attn(q, qpos, k, v, kpos, slopes):
    s = jnp.einsum('btghd,bsgd->bghts', q, k).astype(jnp.float32) * SCALE
    dist = qpos[:, None] - kpos[None, :]
    ok = (dist >= 0) & (dist < NSA_WINDOW) & (kpos[None, :] >= 0)
    s = s - slopes[None, :, :, None, None] * dist.astype(jnp.float32)
    p = masked_softmax(s, ok, -1)
    return jnp.einsum('bghts,bsgd->btghd', p.astype(v.dtype), v)


def window_prompt(q, k, v, slopes):
    B, T = q.shape[:2]
    nqb = T // Q_BLOCK
    nw = NSA_WINDOW // Q_BLOCK

    def band(x):
        xp = jnp.pad(x, ((0, 0), (NSA_WINDOW, 0), (0, 0), (0, 0))).reshape(B, nqb + nw, Q_BLOCK, G_NSA, HEAD_DIM)
        xs = jnp.stack([xp[:, j:j + nqb] for j in range(nw + 1)], axis=2)
        return xs.reshape(B, nqb, (nw + 1) * Q_BLOCK, G_NSA, HEAD_DIM)

    qb = q.reshape(B, nqb, Q_BLOCK, G_NSA, HPG, HEAD_DIM)
    qpos = jnp.arange(T).reshape(nqb, Q_BLOCK)
    kpos = (jnp.arange(nqb) * Q_BLOCK - NSA_WINDOW)[:, None] + jnp.arange((nw + 1) * Q_BLOCK)[None, :]
    o = jax.vmap(window_attn, in_axes=(1, 0, 1, 1, 0, None), out_axes=1)(qb, qpos, band(k), band(v), kpos, slopes)
    return o.reshape(B, T, G_NSA, HPG, HEAD_DIM)


def fox_prompt(q, k, v, logf):
    B, T, H, _ = q.shape
    nqb = T // Q_BLOCK
    c = jnp.cumsum(logf.astype(jnp.float32), axis=1)
    ck = c.transpose(0, 2, 1)
    kpos = jnp.arange(T)

    def block(args):
        qb, cq, qpos = args
        s = (jnp.einsum('bqhd,bkhd->bhqk', qb, k).astype(jnp.float32) * SCALE
             + cq.transpose(0, 2, 1)[..., None] - ck[:, :, None, :])
        p = masked_softmax(s, kpos[None, :] <= qpos[:, None], -1)
        return jnp.einsum('bhqk,bkhd->bqhd', p.astype(v.dtype), v)

    qs = q.reshape(B, nqb, Q_BLOCK, H, HEAD_DIM).transpose(1, 0, 2, 3, 4)
    cs = c.reshape(B, nqb, Q_BLOCK, H).transpose(1, 0, 2, 3)
    ps = jnp.arange(T).reshape(nqb, Q_BLOCK)
    o = lax.map(block, (qs, cs, ps))
    return o.transpose(1, 0, 2, 3, 4).reshape(B, T, H, HEAD_DIM)


def fox_sample(q, k_new, v_new, logf_new, k_past, v_past, logf_past):
    T = q.shape[1]
    P = k_past.shape[1]
    c_past = jnp.cumsum(logf_past.astype(jnp.float32), axis=1)
    c_new = c_past[:, -1:] + jnp.cumsum(logf_new.astype(jnp.float32), axis=1)
    cq = c_new.transpose(0, 2, 1)[..., None]
    s_p = jnp.einsum('bqhd,bkhd->bhqk', q, k_past).astype(jnp.float32) * SCALE + cq - c_past.transpose(0, 2, 1)[:, :, None, :]
    s_n = jnp.einsum('bqhd,bkhd->bhqk', q, k_new).astype(jnp.float32) * SCALE + cq - c_new.transpose(0, 2, 1)[:, :, None, :]
    s = jnp.concatenate([s_p, s_n], axis=-1)
    mask = jnp.concatenate([jnp.ones((T, P), bool), jnp.tril(jnp.ones((T, T), bool))], axis=-1)
    p = masked_softmax(s, mask, -1)
    return (jnp.einsum('bhqk,bkhd->bqhd', p[..., :P].astype(v_past.dtype), v_past)
            + jnp.einsum('bhqk,bkhd->bqhd', p[..., P:].astype(v_new.dtype), v_new))


def mix_out(gate, o_cmp, o_slc, o_win, o_fox, g_group, w_out):
    B, T = gate.shape[:2]
    o_n = gate[:, :, 0, ..., None] * o_cmp + gate[:, :, 1, ..., None] * o_slc + gate[:, :, 2, ..., None] * o_win
    o_n = rms_norm(o_n.reshape(B, T, W_NSA_Q), g_group[:W_NSA_Q])
    o_f = rms_norm(o_fox.reshape(B, T, W_FOX_Q), g_group[W_NSA_Q:])
    return jnp.concatenate([o_n, o_f], axis=-1) @ w_out


def mixer_prompt(h, slopes, w_in, b_gate, b_forget, cmp_pe, w_cmp_k, w_cmp_v, g_group, w_out):
    B, T, _ = h.shape
    q_n, kv_n, gate, q_f, kv_f, logf = project(h, w_in, b_gate, b_forget)
    kc, vc = compress(kv_n[:, :, 0], kv_n[:, :, 1], cmp_pe, w_cmp_k, w_cmp_v)
    ns = -(-T // NSA_BLOCK)
    ksb = to_blocks(kv_n[:, :, 2], ns)
    vsb = to_blocks(kv_n[:, :, 3], ns)
    nqb = T // Q_BLOCK
    qs = q_n.reshape(B, nqb, Q_BLOCK, G_NSA, HPG, HEAD_DIM).transpose(1, 0, 2, 3, 4, 5)
    ps = jnp.arange(T).reshape(nqb, Q_BLOCK)
    o_cmp, o_slc = lax.map(lambda a: nsa_sparse(a[0], a[1], kc, vc, ksb, vsb, slopes), (qs, ps))
    o_cmp = o_cmp.transpose(1, 0, 2, 3, 4, 5).reshape(B, T, G_NSA, HPG, HEAD_DIM)
    o_slc = o_slc.transpose(1, 0, 2, 3, 4, 5).reshape(B, T, G_NSA, HPG, HEAD_DIM)
    o_win = window_prompt(q_n, kv_n[:, :, 4], kv_n[:, :, 5], slopes)
    o_fox = fox_prompt(q_f, kv_f[:, :, 0], kv_f[:, :, 1], logf)
    y = mix_out(gate, o_cmp, o_slc, o_win, o_fox, g_group, w_out)
    wb = min(NSA_WINDOW, T)
    return y, kv_n[:, :, :4], kv_n[:, T - wb:, 4:6], kv_f, logf


def mixer_sample(h, past_nsa, win_buf, past_fox, past_logf, slopes,
                 w_in, b_gate, b_forget, cmp_pe, w_cmp_k, w_cmp_v, g_group, w_out):
    T = h.shape[1]
    past = past_nsa.shape[1]
    q_n, kv_n, gate, q_f, kv_f, logf = project(h, w_in, b_gate, b_forget)
    full = jnp.concatenate([past_nsa, kv_n[:, :, :4].astype(past_nsa.dtype)], axis=1)
    L = past + T
    kc, vc = compress(full[:, :, 0], full[:, :, 1], cmp_pe, w_cmp_k, w_cmp_v)
    ns = -(-L // NSA_BLOCK)
    ksb = to_blocks(full[:, :, 2], ns)
    vsb = to_blocks(full[:, :, 3], ns)
    qpos = past + jnp.arange(T)
    o_cmp, o_slc = nsa_sparse(q_n, qpos, kc, vc, ksb, vsb, slopes)
    wb = win_buf.shape[1]
    kw = jnp.concatenate([win_buf, kv_n[:, :, 4:6].astype(win_buf.dtype)], axis=1)
    kpos = past - wb + jnp.arange(wb + T)
    o_win = window_attn(q_n, qpos, kw[:, :, 0], kw[:, :, 1], kpos, slopes)
    o_fox = fox_sample(q_f, kv_f[:, :, 0], kv_f[:, :, 1], logf,
                       past_fox[:, :, 0], past_fox[:, :, 1], past_logf)
    y = mix_out(gate, o_cmp, o_slc, o_win, o_fox, g_group, w_out)
    return y, kv_n[:, :, :4], kw[:, T:], kv_f, logf


def conv_ffn(h, buf, w_in, cw, cb, w_out):
    T = h.shape[1]
    a, b = jnp.split(h @ w_in, 2, axis=-1)
    ap = jnp.concatenate([buf.astype(a.dtype), a], axis=1)
    ac = cb
    for i in range(CONV_W):
        ac = ac + cw[i] * ap[:, i:i + T]
    return (jax.nn.gelu(ac) * b) @ w_out, ap[:, T:]


def setup_inputs(seed: int = 0) -> dict:
    key = jax.random.key(seed)
    ks = jax.random.split(key, 32)
    n_pages = PAST_LEN // PAGE_SIZE
    n_used = DEC_BATCH * n_pages
    n_pool = n_used + max(n_used // 4, 1)
    wb = min(NSA_WINDOW, PAST_LEN)

    def nrm(k, shape, scale=1.0):
        return scale * jax.random.normal(k, shape, jnp.float32)

    page_table = jax.random.permutation(ks[0], n_pool)[:n_used].reshape(DEC_BATCH, n_pages).astype(jnp.int32)
    return {
        "x_prompt": nrm(ks[1], (BATCH, SEQ, D_MODEL)),
        "x_sample": nrm(ks[2], (DEC_BATCH, DEC_SEQ, D_MODEL)),
        "cache_nsa_kv": nrm(ks[3], (DEPTH, n_pool, PAGE_SIZE, 4, G_NSA, HEAD_DIM)),
        "state_nsa_win": nrm(ks[4], (DEPTH, DEC_BATCH, wb, 2, G_NSA, HEAD_DIM)),
        "cache_fox_kv": nrm(ks[5], (DEPTH, n_pool, PAGE_SIZE, 2, H_FOX, HEAD_DIM)),
        "cache_fox_logf": jax.nn.log_sigmoid(FORGET_BIAS_MEAN + nrm(ks[6], (DEPTH, n_pool, PAGE_SIZE, H_FOX))),
        "state_conv": nrm(ks[7], (DEPTH, DEC_BATCH, CONV_W - 1, D_FF)),
        "page_table": page_table,
        "g_attn_pre": 1.0 + nrm(ks[8], (DEPTH, D_MODEL), 0.05),
        "g_attn_post": 1.0 + nrm(ks[9], (DEPTH, D_MODEL), 0.05),
        "g_ffn_pre": 1.0 + nrm(ks[10], (DEPTH, D_MODEL), 0.05),
        "g_ffn_post": 1.0 + nrm(ks[11], (DEPTH, D_MODEL), 0.05),
        "w_mix_in": nrm(ks[12], (DEPTH, D_MODEL, MIX_IN), D_MODEL ** -0.5),
        "b_gate": nrm(ks[13], (DEPTH, W_NSA_GATE), 0.1),
        "b_forget": FORGET_BIAS_MEAN + nrm(ks[14], (DEPTH, H_FOX), 0.5),
        "cmp_pe": nrm(ks[15], (DEPTH, NSA_BLOCK, HEAD_DIM), 0.5),
        "w_cmp_k": nrm(ks[16], (DEPTH, NSA_BLOCK, HEAD_DIM, HEAD_DIM), (NSA_BLOCK * HEAD_DIM) ** -0.5),
        "w_cmp_v": nrm(ks[17], (DEPTH, NSA_BLOCK, HEAD_DIM, HEAD_DIM), (NSA_BLOCK * HEAD_DIM) ** -0.5),
        "g_group": 1.0 + nrm(ks[18], (DEPTH, MIX_WIDTH), 0.05),
        "w_mix_out": nrm(ks[19], (DEPTH, MIX_WIDTH, D_MODEL), MIX_WIDTH ** -0.5),
        "w_ffn_in": nrm(ks[20], (DEPTH, D_MODEL, 2 * D_FF), D_MODEL ** -0.5),
        "conv_w": nrm(ks[21], (DEPTH, CONV_W, D_FF), CONV_W ** -0.5),
        "conv_b": nrm(ks[22], (DEPTH, D_FF), 0.02),
        "w_ffn_out": nrm(ks[23], (DEPTH, D_FF, D_MODEL), D_FF ** -0.5),
    }


def reference(x_prompt, x_sample, cache_nsa_kv, state_nsa_win, cache_fox_kv, cache_fox_logf, state_conv,
              page_table, g_attn_pre, g_attn_post, g_ffn_pre, g_ffn_post, w_mix_in, b_gate, b_forget,
              cmp_pe, w_cmp_k, w_cmp_v, g_group, w_mix_out, w_ffn_in, conv_w, conv_b, w_ffn_out):
    slopes = alibi_slopes()
    n_dec, n_pages = page_table.shape
    page = cache_nsa_kv.shape[2]
    past = n_pages * page
    xp, xs = x_prompt, x_sample
    p_nsa, s_nsa, p_win, s_win, p_fox, s_fox, p_logf, s_logf, p_conv, s_conv = ([] for _ in range(10))
    for l in range(DEPTH):
        mix_w = (w_mix_in[l], b_gate[l], b_forget[l], cmp_pe[l], w_cmp_k[l], w_cmp_v[l], g_group[l], w_mix_out[l])
        ffn_w = (w_ffn_in[l], conv_w[l], conv_b[l], w_ffn_out[l])
        y, nsa_rows, win_rows, fox_rows, logf_rows = mixer_prompt(rms_norm(xp, g_attn_pre[l]), slopes, *mix_w)
        xp = xp + rms_norm(y, g_attn_post[l])
        zero_buf = jnp.zeros((xp.shape[0], CONV_W - 1, D_FF), xp.dtype)
        f, conv_rows = conv_ffn(rms_norm(xp, g_ffn_pre[l]), zero_buf, *ffn_w)
        xp = xp + rms_norm(f, g_ffn_post[l])
        p_nsa.append(nsa_rows); p_win.append(win_rows); p_fox.append(fox_rows)
        p_logf.append(logf_rows); p_conv.append(conv_rows)
        past_nsa = cache_nsa_kv[l, page_table].reshape(n_dec, past, 4, G_NSA, HEAD_DIM)
        past_fox = cache_fox_kv[l, page_table].reshape(n_dec, past, 2, H_FOX, HEAD_DIM)
        past_logf = cache_fox_logf[l, page_table].reshape(n_dec, past, H_FOX)
        y, nsa_rows, win_rows, fox_rows, logf_rows = mixer_sample(
            rms_norm(xs, g_attn_pre[l]), past_nsa, state_nsa_win[l], past_fox, past_logf, slopes, *mix_w)
        xs = xs + rms_norm(y, g_attn_post[l])
        f, conv_rows = conv_ffn(rms_norm(xs, g_ffn_pre[l]), state_conv[l], *ffn_w)
        xs = xs + rms_norm(f, g_ffn_post[l])
        s_nsa.append(nsa_rows); s_win.append(win_rows); s_fox.append(fox_rows)
        s_logf.append(logf_rows); s_conv.append(conv_rows)
    return (xp, xs,
            jnp.stack(p_nsa), jnp.stack(s_nsa),
            jnp.stack(p_win), jnp.stack(s_win),
            jnp.stack(p_fox), jnp.stack(s_fox),
            jnp.stack(p_logf), jnp.stack(s_logf),
            jnp.stack(p_conv), jnp.stack(s_conv))
```

```python
import functools

import jax
import jax.numpy as jnp
import numpy as np
from jax import lax
from jax.experimental import pallas as pl
from jax.experimental.pallas import tpu as pltpu

F32 = jnp.float32
BF16 = jnp.bfloat16

HEAD_DIM = 64
G_NSA = 2
HPG = 4
H_NSA = G_NSA * HPG
H_FOX = 8
NSA_BLOCK = 64
NSA_TOPK = 16
NSA_WINDOW = 512
CONV_W = 3
RMS_EPS = 1e-6
FORCE_SCORE = 1e4
SCALE = HEAD_DIM ** -0.5

LANE = 128
VMEM_LIMIT = 56 * 1024 * 1024
NEG = -1e30
BIG = 2.0 ** 100
M_INIT = -1e38

W_Q = H_NSA * HEAD_DIM
W_KVN = 6 * G_NSA * HEAD_DIM
W_GATE = 3 * H_NSA
W_QF = H_FOX * HEAD_DIM
W_KVF = 2 * H_FOX * HEAD_DIM
C_QN, C_KVN, C_QF, C_KVF, C_SMALL = 0, 512, 1280, 1792, 2816
N_PROJ = C_SMALL + LANE
LOGF_LANE = W_GATE


def _cparams(sem):
    return pltpu.CompilerParams(dimension_semantics=sem, vmem_limit_bytes=VMEM_LIMIT)


def _nt(a, b):
    return lax.dot_general(a, b, (((1,), (1,)), ((), ())), preferred_element_type=F32)


def _rms(x, g):
    return x * lax.rsqrt(jnp.mean(x * x, axis=-1, keepdims=True) + RMS_EPS) * g


def _proj_kernel(x_ref, g_ref, w_ref, b_ref, qn_ref, nsakv_ref, winf_ref, slcb_ref, winb_ref,
                 qf_ref, kvf_ref, kvfb_ref, small_ref):
    h = _rms(x_ref[...], g_ref[...]).astype(BF16)

    def mm(c0, c1):
        return jnp.dot(h, w_ref[:, c0:c1], preferred_element_type=F32)

    qn_ref[...] = (mm(C_QN, C_QN + W_Q) * SCALE).astype(BF16)
    u = mm(C_KVN, C_KVN + 512)
    nsakv_ref[...] = u
    slcb_ref[...] = u[:, 256:512].astype(BF16)
    u = mm(C_KVN + 512, C_KVN + 768)
    winf_ref[...] = u
    winb_ref[...] = u.astype(BF16)
    qf_ref[...] = (mm(C_QF, C_QF + W_QF) * SCALE).astype(BF16)
    u = mm(C_KVF, C_KVF + W_KVF)
    kvf_ref[...] = u
    kvfb_ref[...] = u.astype(BF16)
    z = mm(C_SMALL, N_PROJ) + b_ref[...]
    lane = lax.broadcasted_iota(jnp.int32, z.shape, 1)
    log_sig = jnp.minimum(z, 0.0) - jnp.log(1.0 + jnp.exp(-jnp.abs(z)))
    small_ref[...] = jnp.where(lane < W_GATE, jax.nn.sigmoid(z), log_sig)


def _proj(x, g, w, brow, tm):
    m, d = x.shape
    widths = [(W_Q, BF16), (512, F32), (256, F32), (256, BF16), (256, BF16), (W_QF, BF16),
              (W_KVF, F32), (W_KVF, BF16), (LANE, F32)]
    return pl.pallas_call(
        _proj_kernel,
        grid=(m // tm,),
        in_specs=[pl.BlockSpec((tm, d), lambda i: (i, 0)),
                  pl.BlockSpec((1, d), lambda i: (0, 0)),
                  pl.BlockSpec((d, N_PROJ), lambda i: (0, 0)),
                  pl.BlockSpec((1, LANE), lambda i: (0, 0))],
        out_specs=[pl.BlockSpec((tm, wd), lambda i: (i, 0)) for wd, _ in widths],
        out_shape=[jax.ShapeDtypeStruct((m, wd), dt) for wd, dt in widths],
        compiler_params=_cparams(("parallel",)),
        name="proj",
    )(x, g, w, brow)


def _compress_kernel(x_ref, pe_ref, w_ref, o_ref, acc_ref, accpe_ref):
    k = pl.program_id(1)

    @pl.when(k == 0)
    def _():
        acc_ref[...] = jnp.zeros_like(acc_ref)
        accpe_ref[...] = jnp.zeros_like(accpe_ref)

    w = w_ref[...]
    acc_ref[...] += jnp.dot(x_ref[...].astype(BF16), w, preferred_element_type=F32)
    accpe_ref[...] += jnp.dot(pe_ref[...], w, preferred_element_type=F32)

    @pl.when(k == pl.num_programs(1) - 1)
    def _():
        o_ref[...] = (acc_ref[...] + accpe_ref[0:1, :]).astype(BF16)


def _compress(x, pe_rows, w_big, tmb, tkc):
    nb, kk = x.shape
    return pl.pallas_call(
        _compress_kernel,
        grid=(nb // tmb, kk // tkc),
        in_specs=[pl.BlockSpec((tmb, tkc), lambda i, k: (i, k)),
                  pl.BlockSpec((8, tkc), lambda i, k: (0, k)),
                  pl.BlockSpec((tkc, 256), lambda i, k: (k, 0))],
        out_specs=pl.BlockSpec((tmb, 256), lambda i, k: (i, 0)),
        out_shape=jax.ShapeDtypeStruct((nb, 256), BF16),
        scratch_shapes=[pltpu.VMEM((tmb, 256), F32), pltpu.VMEM((8, 256), F32)],
        compiler_params=_cparams(("parallel", "arbitrary")),
        name="compress",
    )(x, pe_rows, w_big)


def _online_update(s, v, m_ref, l_ref, acc_ref):
    m_prev = m_ref[...]
    m_new = jnp.maximum(m_prev, jnp.max(s, axis=-1, keepdims=True))
    alpha = jnp.exp(m_prev - m_new)
    p = jnp.exp(s - m_new)
    l_ref[...] = alpha * l_ref[...] + jnp.sum(p, axis=-1, keepdims=True)
    acc_ref[...] = alpha * acc_ref[...] + jnp.dot(p.astype(BF16), v, preferred_element_type=F32)
    m_ref[...] = m_new


def _masked_softmax(s, mask):
    s = jnp.where(mask, s, NEG)
    m = jnp.max(s, axis=-1, keepdims=True)
    e = jnp.where(mask, jnp.exp(s - m), 0.0)
    d = jnp.sum(e, axis=-1, keepdims=True)
    return e / jnp.maximum(d, 1e-30)


def _topk_select(score, blkid, k):
    sel = jnp.zeros_like(score)
    big = jnp.int32(1 << 30)
    for _ in range(k):
        mx = jnp.max(score, axis=-1, keepdims=True)
        idx = jnp.min(jnp.where(score == mx, blkid, big), axis=-1, keepdims=True)
        hit = blkid == idx
        sel = jnp.where(hit, jnp.where(mx >= 0.0, 1.0, 0.0), sel)
        score = jnp.where(hit, -3e38, score)
    return sel


def _slope(g, j):
    return 2.0 ** (-(g * HPG + j + 1))


def _build_q_aug(q_ref, qa_ref, qpos, tq):
    lane = lax.broadcasted_iota(jnp.int32, (tq, LANE), 1)
    qblk = (qpos >> 6).astype(F32)
    qin = (qpos & 63).astype(F32)
    for g in range(G_NSA):
        for j in range(HPG):
            sl = _slope(g, j)
            qb = q_ref[0, :, j * LANE:(j + 1) * LANE]
            qm = jnp.where((lane >> 6) == g, qb, jnp.zeros_like(qb))
            aug = jnp.where(lane == 0, sl * NSA_BLOCK,
                            jnp.where(lane == 1, sl,
                                      jnp.where(lane == 2, -sl * NSA_BLOCK * qblk,
                                                jnp.where(lane == 3, -sl * qin, 0.0))))
            r0 = (g * HPG + j) * tq
            qa_ref[r0:r0 + tq, :] = jnp.concatenate([qm, aug.astype(BF16)], axis=1)


def _nsa_prompt_kernel(q_ref, gate_ref, kvc_ref, cpos_ref, slc_ref, win_ref, kpos_ref, ebt_ref, o_ref,
                       qa_ref, selm_ref, m_ref, l_ref, acc_ref, *, tq, tk, seq, nblk):
    qi = pl.program_id(1)
    q0 = qi * tq
    nh = H_NSA
    row = lax.broadcasted_iota(jnp.int32, (tq, LANE), 0)
    lane = lax.broadcasted_iota(jnp.int32, (tq, LANE), 1)
    qpos = q0 + row
    _build_q_aug(q_ref, qa_ref, qpos, tq)
    qa = qa_ref[...]

    kca = jnp.concatenate([kvc_ref[0, :, 0:LANE], cpos_ref[...]], axis=1)
    s = _nt(qa, kca).reshape(nh, tq, LANE)
    vis = (lane * NSA_BLOCK + (NSA_BLOCK - 1) <= qpos) & (lane < nblk)
    p = _masked_softmax(s, vis[None])
    o_cmp = jnp.dot(p.reshape(nh * tq, LANE).astype(BF16), kvc_ref[0, :, LANE:2 * LANE],
                    preferred_element_type=F32).reshape(nh, tq, LANE)

    cur = qpos >> 6
    forced = (lane == 0) | (lane == cur) | (lane == cur - 1)
    for g in range(G_NSA):
        imp = p[g * HPG] + p[g * HPG + 1] + p[g * HPG + 2] + p[g * HPG + 3]
        score = jnp.where(lane <= cur, jnp.where(forced, FORCE_SCORE, imp), -1.0)
        sel = _topk_select(score, lane, NSA_TOPK)
        selm_ref[g] = (sel - 1.0).astype(BF16)

    m_ref[...] = jnp.full_like(m_ref, M_INIT)
    l_ref[...] = jnp.zeros_like(l_ref)
    acc_ref[...] = jnp.zeros_like(acc_ref)
    kt_last = (q0 + tq - 1) // tk

    def tile(kt, causal):
        k0 = pl.multiple_of(kt * tk, tk)
        ka = jnp.concatenate([slc_ref[0, pl.ds(k0, tk), 0:LANE], kpos_ref[pl.ds(k0, tk), :]], axis=1)
        s2 = _nt(qa_ref[...], ka).reshape(G_NSA, HPG, tq, tk)
        ebt = ebt_ref[pl.ds(k0, tk), :]
        mbs = []
        for g in range(G_NSA):
            mb = _nt(selm_ref[g], ebt)
            if causal:
                kp = k0 + lax.broadcasted_iota(jnp.int32, (tq, tk), 1)
                qp = q0 + lax.broadcasted_iota(jnp.int32, (tq, tk), 0)
                mb = jnp.where(kp <= qp, mb, -BIG)
            mbs.append(mb)
        s2 = s2 + jnp.stack(mbs)[:, None]
        _online_update(s2.reshape(nh * tq, tk), slc_ref[0, pl.ds(k0, tk), LANE:2 * LANE], m_ref, l_ref, acc_ref)

    def body(kt, carry):
        tile(kt, False)
        return carry

    lax.fori_loop(0, kt_last, body, 0)
    tile(kt_last, True)
    o_slc = (acc_ref[...] / l_ref[...]).reshape(nh, tq, LANE)

    wk = NSA_WINDOW + tq
    kstart = pl.multiple_of(jnp.clip(q0 - NSA_WINDOW, 0, seq - wk), 8)
    ka = jnp.concatenate([win_ref[0, pl.ds(kstart, wk), 0:LANE], kpos_ref[pl.ds(kstart, wk), :]], axis=1)
    s3 = _nt(qa_ref[...], ka).reshape(nh, tq, wk)
    dist = (q0 - kstart) + (lax.broadcasted_iota(jnp.int32, (tq, wk), 0)
                            - lax.broadcasted_iota(jnp.int32, (tq, wk), 1))
    okw = (dist >= 0) & (dist < NSA_WINDOW)
    pw = _masked_softmax(s3, okw[None])
    o_win = jnp.dot(pw.reshape(nh * tq, wk).astype(BF16), win_ref[0, pl.ds(kstart, wk), LANE:2 * LANE],
                    preferred_element_type=F32).reshape(nh, tq, LANE)

    gate = gate_ref[0]
    for j in range(HPG):
        parts = []
        for g in range(G_NSA):
            hh = g * HPG + j
            c = 2 * j + g
            parts.append(gate[:, c:c + 1] * o_cmp[hh] + gate[:, 8 + c:9 + c] * o_slc[hh]
                         + gate[:, 16 + c:17 + c] * o_win[hh])
        o_ref[0, :, j * LANE:(j + 1) * LANE] = jnp.where(lane < HEAD_DIM, parts[0], parts[1])


def _nsa_prompt(qn, small, kvc, cpos, slcb, winb, kpos, ebt, tq, tk):
    b, seq, _ = qn.shape
    nblk = seq // NSA_BLOCK
    kern = functools.partial(_nsa_prompt_kernel, tq=tq, tk=tk, seq=seq, nblk=nblk)
    rows = H_NSA * tq
    return pl.pallas_call(
        kern,
        grid=(b, seq // tq),
        in_specs=[pl.BlockSpec((1, tq, W_Q), lambda i, j: (i, j, 0)),
                  pl.BlockSpec((1, tq, LANE), lambda i, j: (i, j, 0)),
                  pl.BlockSpec((1, LANE, 256), lambda i, j: (i, 0, 0)),
                  pl.BlockSpec((LANE, LANE), lambda i, j: (0, 0)),
                  pl.BlockSpec((1, seq, 256), lambda i, j: (i, 0, 0)),
                  pl.BlockSpec((1, seq, 256), lambda i, j: (i, 0, 0)),
                  pl.BlockSpec((seq, LANE), lambda i, j: (0, 0)),
                  pl.BlockSpec((seq, LANE), lambda i, j: (0, 0))],
        out_specs=pl.BlockSpec((1, tq, W_Q), lambda i, j: (i, j, 0)),
        out_shape=jax.ShapeDtypeStruct((b, seq, W_Q), F32),
        scratch_shapes=[pltpu.VMEM((rows, 2 * LANE), BF16),
                        pltpu.VMEM((G_NSA, tq, LANE), BF16),
                        pltpu.VMEM((rows, 1), F32),
                        pltpu.VMEM((rows, 1), F32),
                        pltpu.VMEM((rows, LANE), F32)],
        compiler_params=_cparams(("parallel", "parallel")),
        name="nsa_prompt",
    )(qn, small, kvc, cpos, slcb, winb, kpos, ebt)


def _fox_prep_kernel(lf_ref, ltri_ref, place_ref, o_ref, carry_ref):
    t = pl.program_id(1)

    @pl.when(t == 0)
    def _():
        carry_ref[...] = jnp.zeros_like(carry_ref)

    def split3(x):
        hi = x.astype(BF16)
        r = x - hi.astype(F32)
        mid = r.astype(BF16)
        return hi, mid, (r - mid.astype(F32)).astype(BF16)

    ltri = ltri_ref[...]
    c = carry_ref[...]
    for part in split3(lf_ref[0]):
        c = c + jnp.dot(ltri, part, preferred_element_type=F32)
    tc = c.shape[0]
    carry_ref[...] = c[tc - 1:tc, :]
    out = jnp.zeros((tc, LANE), F32)
    for i, part in enumerate(split3(c)):
        out = out + jnp.dot(part, place_ref[i], preferred_element_type=F32)
    o_ref[0] = out.astype(BF16)


def _fox_prep(lf, ltri, place, tc):
    b, n, _ = lf.shape
    return pl.pallas_call(
        _fox_prep_kernel,
        grid=(b, n // tc),
        in_specs=[pl.BlockSpec((1, tc, LANE), lambda i, t: (i, t, 0)),
                  pl.BlockSpec((tc, tc), lambda i, t: (0, 0)),
                  pl.BlockSpec((3, LANE, LANE), lambda i, t: (0, 0, 0))],
        out_specs=pl.BlockSpec((1, tc, LANE), lambda i, t: (i, t, 0)),
        out_shape=jax.ShapeDtypeStruct((b, n, LANE), BF16),
        scratch_shapes=[pltpu.VMEM((1, LANE), F32)],
        compiler_params=_cparams(("parallel", "arbitrary")),
        name="fox_prep",
    )(lf, ltri, place)


def _qf_aug(q, pair, tq):
    lane = lax.broadcasted_iota(jnp.int32, (tq, LANE), 1)
    q = q.astype(F32)
    parts = []
    for e in range(2):
        qm = jnp.where((lane >> 6) == e, q, 0.0)
        d = lane - (8 * pair + e)
        aug = jnp.where(d == 0, -1.0, jnp.where(d == 2, -1.0, jnp.where(d == 4, -1.0, 0.0)))
        parts.append(jnp.concatenate([qm, aug], axis=1))
    return jnp.concatenate(parts, axis=0)


def _fox_prompt_kernel(q_ref, k_ref, v_ref, ca_ref, o_ref, qa_ref, m_ref, l_ref, acc_ref, *, tq):
    pair = pl.program_id(1)
    qi = pl.program_id(2)
    qa_ref[...] = _qf_aug(q_ref[0], pair, tq).astype(BF16)
    m_ref[...] = jnp.full_like(m_ref, M_INIT)
    l_ref[...] = jnp.zeros_like(l_ref)
    acc_ref[...] = jnp.zeros_like(acc_ref)

    def tile(kt, causal):
        k0 = pl.multiple_of(kt * tq, tq)
        ka = jnp.concatenate([k_ref[0, pl.ds(k0, tq), :], ca_ref[0, pl.ds(k0, tq), :]], axis=1)
        s = _nt(qa_ref[...], ka)
        if causal:
            ok = (lax.broadcasted_iota(jnp.int32, (tq, tq), 1) <= lax.broadcasted_iota(jnp.int32, (tq, tq), 0))
            s = jnp.where(ok[None], s.reshape(2, tq, tq), -BIG).reshape(2 * tq, tq)
        _online_update(s, v_ref[0, pl.ds(k0, tq), :], m_ref, l_ref, acc_ref)

    def body(kt, carry):
        tile(kt, False)
        return carry

    lax.fori_loop(0, qi, body, 0)
    tile(qi, True)
    o = (acc_ref[...] / l_ref[...]).reshape(2, tq, LANE)
    lane = lax.broadcasted_iota(jnp.int32, (tq, LANE), 1)
    o_ref[0] = jnp.where(lane < HEAD_DIM, o[0], o[1])


def _fox_prompt(qf, kvfb, caug, tq):
    b, seq, _ = qf.shape
    npair = H_FOX // 2
    return pl.pallas_call(
        functools.partial(_fox_prompt_kernel, tq=tq),
        grid=(b, npair, seq // tq),
        in_specs=[pl.BlockSpec((1, tq, LANE), lambda i, p, j: (i, j, p)),
                  pl.BlockSpec((1, seq, LANE), lambda i, p, j: (i, 0, p)),
                  pl.BlockSpec((1, seq, LANE), lambda i, p, j: (i, 0, npair + p)),
                  pl.BlockSpec((1, seq, LANE), lambda i, p, j: (i, 0, 0))],
        out_specs=pl.BlockSpec((1, tq, LANE), lambda i, p, j: (i, j, p)),
        out_shape=jax.ShapeDtypeStruct((b, seq, W_QF), F32),
        scratch_shapes=[pltpu.VMEM((2 * tq, 2 * LANE), BF16),
                        pltpu.VMEM((2 * tq, 1), F32),
                        pltpu.VMEM((2 * tq, 1), F32),
                        pltpu.VMEM((2 * tq, LANE), F32)],
        compiler_params=_cparams(("parallel", "parallel", "parallel")),
        name="fox_prompt",
    )(qf, kvfb, kvfb, caug)


def _mixout_kernel(on_ref, of_ref, x_ref, gg_ref, w_ref, gp_ref, o_ref):
    gg = gg_ref[...]
    half = on_ref.shape[1]
    yn = _rms(on_ref[...], gg[:, :half])
    yf = _rms(of_ref[...], gg[:, half:])
    y = jnp.dot(jnp.concatenate([yn, yf], axis=1).astype(BF16), w_ref[...], preferred_element_type=F32)
    o_ref[...] = x_ref[...] + _rms(y, gp_ref[...])


def _mixout(on, of, x, gg, w, gp, tm):
    m, d = x.shape
    half = on.shape[1]
    return pl.pallas_call(
        _mixout_kernel,
        grid=(m // tm,),
        in_specs=[pl.BlockSpec((tm, half), lambda i: (i, 0)),
                  pl.BlockSpec((tm, half), lambda i: (i, 0)),
                  pl.BlockSpec((tm, d), lambda i: (i, 0)),
                  pl.BlockSpec((1, 2 * half), lambda i: (0, 0)),
                  pl.BlockSpec((2 * half, d), lambda i: (0, 0)),
                  pl.BlockSpec((1, d), lambda i: (0, 0))],
        out_specs=pl.BlockSpec((tm, d), lambda i: (i, 0)),
        out_shape=jax.ShapeDtypeStruct((m, d), F32),
        compiler_params=_cparams(("parallel",)),
        name="mixout",
    )(on, of, x, gg, w, gp)


def _gelu_tanh(x):
    return 0.5 * x * (1.0 + jnp.tanh(np.sqrt(2.0 / np.pi) * (x + 0.044715 * (x * x * x))))


def _ffn_kernel(*refs, tm, seq, has_state, tail):
    if has_state:
        (x_ref, xh_ref, st_ref, g_ref, wa_ref, wb_ref, cw_ref, cb_ref, wo_ref, gp_ref,
         o_ref, a_ref, h_ref, acc_ref) = refs
    else:
        (x_ref, xh_ref, g_ref, wa_ref, wb_ref, cw_ref, cb_ref, wo_ref, gp_ref,
         o_ref, a_ref, h_ref, acc_ref) = refs
    i = pl.program_id(0)
    j = pl.program_id(1)

    @pl.when(j == 0)
    def _():
        g = g_ref[...]
        h_ref[0:8, :] = _rms(xh_ref[...], g).astype(BF16)
        h_ref[8:, :] = _rms(x_ref[...], g).astype(BF16)
        acc_ref[...] = jnp.zeros_like(acc_ref)

    a_ext = jnp.dot(h_ref[...], wa_ref[...], preferred_element_type=F32)
    rows = lax.broadcasted_iota(jnp.int32, a_ext.shape, 0)
    if has_state:
        a_ext = jnp.where((rows & 7) >= 6, st_ref[...], a_ext)
    else:
        keep = jnp.logical_or(rows >= 8, (i * tm) % seq != 0)
        a_ext = jnp.where(keep, a_ext, 0.0)
    a = a_ext[8:, :]
    a1 = pltpu.roll(a_ext, 1, 0)[8:, :]
    a2 = pltpu.roll(a_ext, 2, 0)[8:, :]
    cw = cw_ref[...]
    ac = cb_ref[...] + cw[0:1, :] * a2 + cw[1:2, :] * a1 + cw[2:3, :] * a
    b = jnp.dot(h_ref[8:, :], wb_ref[...], preferred_element_type=F32)
    acc_ref[...] += jnp.dot((_gelu_tanh(ac) * b).astype(BF16), wo_ref[...], preferred_element_type=F32)
    a_ref[0] = a[tm - tail:, :]

    @pl.when(j == pl.num_programs(1) - 1)
    def _():
        o_ref[...] = x_ref[...] + _rms(acc_ref[...], gp_ref[...])


def _ffn(x, state_ext, g, wa, wb, cw, cb, wo, gp, tm, tf, seq, tail):
    m, d = x.shape
    dff = wa.shape[1]
    has_state = state_ext is not None
    nb = m // tm
    halo = lambda i, j: (jnp.maximum(i * (tm // 8) - 1, 0), 0)
    in_specs = [pl.BlockSpec((tm, d), lambda i, j: (i, 0)), pl.BlockSpec((8, d), halo)]
    args = [x, x]
    if has_state:
        in_specs.append(pl.BlockSpec((tm + 8, tf), lambda i, j: (0, j)))
        args.append(state_ext)
    in_specs += [pl.BlockSpec((1, d), lambda i, j: (0, 0)),
                 pl.BlockSpec((d, tf), lambda i, j: (0, j)),
                 pl.BlockSpec((d, tf), lambda i, j: (0, j)),
                 pl.BlockSpec((CONV_W, tf), lambda i, j: (0, j)),
                 pl.BlockSpec((1, tf), lambda i, j: (0, j)),
                 pl.BlockSpec((tf, d), lambda i, j: (j, 0)),
                 pl.BlockSpec((1, d), lambda i, j: (0, 0))]
    args += [g, wa, wb, cw, cb, wo, gp]
    return pl.pallas_call(
        functools.partial(_ffn_kernel, tm=tm, seq=seq, has_state=has_state, tail=tail),
        grid=(nb, dff // tf),
        in_specs=in_specs,
        out_specs=[pl.BlockSpec((tm, d), lambda i, j: (i, 0)),
                   pl.BlockSpec((1, tail, tf), lambda i, j: (i, 0, j))],
        out_shape=[jax.ShapeDtypeStruct((m, d), F32), jax.ShapeDtypeStruct((nb, tail, dff), F32)],
        scratch_shapes=[pltpu.VMEM((tm + 8, d), BF16), pltpu.VMEM((tm, d), F32)],
        compiler_params=_cparams(("parallel", "arbitrary")),
        name="ffn",
    )(*args)


def _page_copy(cache_ref, layer, page, r0, nrows, dst, sem):
    return pltpu.make_async_copy(cache_ref.at[layer, page, pl.ds(r0, nrows), :], dst, sem)


def _cmp_sample_kernel(pt_ref, cn_ref, pet_ref, w_ref, o_ref, cbuf, sem, *, layer, n_pages):
    b = pl.program_id(0)

    def copy(p):
        return _page_copy(cn_ref, layer, pt_ref[b, p], 0, 256, cbuf.at[pl.ds(p * 256, 256), :], sem)

    def start(p, c):
        copy(p).start()
        return c

    def wait(p, c):
        copy(p).wait()
        return c

    lax.fori_loop(0, n_pages, start, 0)
    lax.fori_loop(0, n_pages, wait, 0)

    for c in range(2):
        def body(d, acc):
            pe = pet_ref[pl.ds(d, 1), :]
            rows = [cbuf[pl.ds(c * 128 + g * 64 + d, n_pages, stride=256), :] for g in range(G_NSA)]
            lhs = (jnp.concatenate(rows, axis=0) + pe).astype(BF16)
            return acc + jnp.dot(lhs, w_ref[c, d], preferred_element_type=F32)

        acc = lax.fori_loop(0, HEAD_DIM, body, jnp.zeros((G_NSA * n_pages, LANE), F32), unroll=4)
        o_ref[0, c] = acc.astype(BF16)


def _cmp_sample(page_table, cn, pet, wcd, layer):
    db, n_pages = page_table.shape
    return pl.pallas_call(
        functools.partial(_cmp_sample_kernel, layer=layer, n_pages=n_pages),
        grid_spec=pltpu.PrefetchScalarGridSpec(
            num_scalar_prefetch=1, grid=(db,),
            in_specs=[pl.BlockSpec(memory_space=pl.ANY),
                      pl.BlockSpec((HEAD_DIM, LANE), lambda i, pt: (0, 0)),
                      pl.BlockSpec((2, HEAD_DIM, LANE, LANE), lambda i, pt: (0, 0, 0, 0))],
            out_specs=pl.BlockSpec((1, 2, G_NSA * n_pages, LANE), lambda i, pt: (i, 0, 0, 0)),
            scratch_shapes=[pltpu.VMEM((n_pages * 256, LANE), F32), pltpu.SemaphoreType.DMA(())]),
        out_shape=jax.ShapeDtypeStruct((db, 2, G_NSA * n_pages, LANE), BF16),
        compiler_params=_cparams(("arbitrary",)),
        name="cmp_sample",
    )(page_table, cn, pet, wcd)


def _softmax2(parts):
    m = functools.reduce(jnp.maximum, [jnp.max(s, axis=-1, keepdims=True) for s in parts])
    es = [jnp.exp(s - m) for s in parts]
    d = functools.reduce(jnp.add, [jnp.sum(e, axis=-1, keepdims=True) for e in es])
    return es, d


def _pad_rows(x, n):
    return jnp.concatenate([x, jnp.zeros((n - x.shape[0],) + x.shape[1:], x.dtype)], axis=0)


def _nsa_sample_kernel(pt_ref, q_ref, gate_ref, kc_ref, cn_ref, wst_ref, slcn_ref, winn_ref, eb_ref, o_ref,
                       kbuf, vbuf, sem, *, layer, n_pages, past, dseq, rp):
    b = pl.program_id(0)
    nh = H_NSA
    r = nh * rp

    def copies(p):
        page = pt_ref[b, p]
        win = pl.ds(pl.multiple_of(p * LANE, LANE), LANE)
        return (_page_copy(cn_ref, layer, page, 256, 128, kbuf.at[:, win], sem),
                _page_copy(cn_ref, layer, page, 384, 128, vbuf.at[:, win], sem))

    def start(p, c):
        for cp in copies(p):
            cp.start()
        return c

    def wait(p, c):
        for cp in copies(p):
            cp.wait()
        return c

    lax.fori_loop(0, n_pages, start, 0)

    lane = lax.broadcasted_iota(jnp.int32, (rp, LANE), 1)
    row = lax.broadcasted_iota(jnp.int32, (rp, LANE), 0)
    tok = jnp.minimum(row, dseq - 1)
    rr = lax.broadcasted_iota(jnp.int32, (r, 1), 0)
    hrow = rr >> 3
    grow = hrow >> 2
    slope = lax.bitcast_convert_type((126 - hrow) << 23, F32)
    qpos_r = past + jnp.minimum(rr & 7, dseq - 1)

    qs, qsw = [], []
    for g in range(G_NSA):
        for j in range(HPG):
            q32 = q_ref[0, :, j * LANE:(j + 1) * LANE].astype(F32)
            qm = jnp.where((lane >> 6) == g, q32, 0.0)
            qs.append(qm)
            qsw.append(pltpu.roll(qm, HEAD_DIM, 1))
    qm_all = jnp.concatenate(qs, axis=0)
    qsw_all = jnp.concatenate(qsw, axis=0)
    q_lo = jnp.where(grow == 0, qm_all, qsw_all).astype(BF16)
    q_hi = jnp.where(grow == 0, qsw_all, qm_all).astype(BF16)
    qm_bf = qm_all.astype(BF16)

    lane_r = lax.broadcasted_iota(jnp.int32, (r, LANE), 1)
    own = (lane_r >> 6) == grow
    kc = kc_ref[0, 0]
    vc = kc_ref[0, 1]
    s_par, vis_par = [], []
    for par, qq in ((0, q_lo), (1, q_hi)):
        blk = 2 * (lane_r & 63) + par
        dist = qpos_r - (blk * NSA_BLOCK + (NSA_BLOCK - 1))
        s_par.append(_nt(qq, kc) - slope * dist.astype(F32))
        vis_par.append(own & (dist >= 0) & ((lane_r & 63) < n_pages))
    sm = [jnp.where(v, s, NEG) for s, v in zip(s_par, vis_par)]
    mx = jnp.maximum(jnp.max(sm[0], axis=-1, keepdims=True), jnp.max(sm[1], axis=-1, keepdims=True))
    es = [jnp.where(v, jnp.exp(s - mx), 0.0) for s, v in zip(sm, vis_par)]
    den = jnp.maximum(jnp.sum(es[0], axis=-1, keepdims=True) + jnp.sum(es[1], axis=-1, keepdims=True), 1e-30)
    ps = [e / den for e in es]
    a0 = jnp.dot(ps[0].astype(BF16), vc, preferred_element_type=F32)
    a1 = jnp.dot(ps[1].astype(BF16), vc, preferred_element_type=F32)
    cmix = jnp.where(lane_r < HEAD_DIM, a0, a1)
    o_cmp = cmix + pltpu.roll(cmix, HEAD_DIM, 1)

    cur = past // NSA_BLOCK
    selms = []
    for g in range(G_NSA):
        sc, ids, oks = [], [], []
        for par in range(2):
            p4 = ps[par].reshape(nh, rp, LANE)
            imp = p4[g * HPG] + p4[g * HPG + 1] + p4[g * HPG + 2] + p4[g * HPG + 3]
            blk = 2 * (lane & 63) + par
            ok = ((lane >> 6) == g) & ((lane & 63) < n_pages)
            forced = (blk == 0) | (blk == cur - 1)
            sc.append(jnp.where(ok, jnp.where(forced, FORCE_SCORE, imp), -2.0))
            ids.append(blk + jnp.where(ok, 0, 1 << 20) + (lane >> 6) * (1 << 10))
            oks.append(ok)
        sel = _topk_select(jnp.concatenate(sc, axis=1), jnp.concatenate(ids, axis=1), NSA_TOPK - 1)
        selms.append(jnp.where(jnp.concatenate(oks, axis=1), sel - 1.0, 0.0))
    selm = jnp.concatenate(selms, axis=0).astype(BF16)

    lax.fori_loop(0, n_pages, wait, 0)
    nk = n_pages * LANE
    s_p = jnp.dot(qm_bf, kbuf[...].astype(BF16), preferred_element_type=F32)
    kpos = lax.broadcasted_iota(jnp.int32, (r, nk), 1)
    mb = jnp.dot(selm, eb_ref[...], preferred_element_type=F32)
    mb = jnp.broadcast_to(mb.reshape(G_NSA, 1, rp, nk), (G_NSA, HPG, rp, nk)).reshape(r, nk)
    s_p = s_p - slope * (qpos_r - kpos).astype(F32) + mb
    kn = _pad_rows(slcn_ref[0, :, 0:LANE], LANE)
    vn = _pad_rows(slcn_ref[0, :, LANE:2 * LANE], LANE)
    s_n = _nt(qm_bf, kn)
    kposn = past + lane_r
    okn = (lane_r < dseq) & (kposn <= qpos_r)
    s_n = jnp.where(okn, s_n - slope * (qpos_r - kposn).astype(F32), -BIG)
    (e_p, e_n), den = _softmax2([s_p, s_n])
    o_slc = (_nt(e_p.astype(BF16), vbuf[...].astype(BF16))
             + jnp.dot(e_n.astype(BF16), vn, preferred_element_type=F32)) / den

    wbs = wst_ref.shape[3]
    s_w = jnp.dot(qm_bf, wst_ref[0, 0, 0:LANE, :].astype(BF16), preferred_element_type=F32)
    kposw = (past - wbs) + lax.broadcasted_iota(jnp.int32, (r, wbs), 1)
    dw = qpos_r - kposw
    s_w = jnp.where((dw >= 0) & (dw < NSA_WINDOW) & (kposw >= 0), s_w - slope * dw.astype(F32), -BIG)
    kwn = _pad_rows(winn_ref[0, :, 0:LANE], LANE)
    vwn = _pad_rows(winn_ref[0, :, LANE:2 * LANE], LANE)
    dn = qpos_r - kposn
    s_wn = jnp.where((lane_r < dseq) & (dn >= 0) & (dn < NSA_WINDOW), _nt(qm_bf, kwn) - slope * dn.astype(F32), -BIG)
    (e_w, e_wn), denw = _softmax2([s_w, s_wn])
    o_win = (_nt(e_w.astype(BF16), wst_ref[0, 0, LANE:2 * LANE, :].astype(BF16))
             + jnp.dot(e_wn.astype(BF16), vwn, preferred_element_type=F32)) / denw

    gate = gate_ref[0]
    o_cmp = o_cmp.reshape(nh, rp, LANE)
    o_slc = o_slc.reshape(nh, rp, LANE)
    o_win = o_win.reshape(nh, rp, LANE)
    for j in range(HPG):
        parts = []
        for g in range(G_NSA):
            hh = g * HPG + j
            c = 2 * j + g
            parts.append(gate[:, c:c + 1] * o_cmp[hh] + gate[:, 8 + c:9 + c] * o_slc[hh]
                         + gate[:, 16 + c:17 + c] * o_win[hh])
        o_ref[0, :, j * LANE:(j + 1) * LANE] = jnp.where(lane < HEAD_DIM, parts[0], parts[1])


def _nsa_sample(page_table, qn, small, kcs, cn, wst, slcn, winn, eb, layer, past, dseq):
    db, n_pages = page_table.shape
    rp = qn.shape[1]
    nk = n_pages * LANE
    wbs = wst.shape[3]
    kern = functools.partial(_nsa_sample_kernel, layer=layer, n_pages=n_pages, past=past, dseq=dseq, rp=rp)
    return pl.pallas_call(
        kern,
        grid_spec=pltpu.PrefetchScalarGridSpec(
            num_scalar_prefetch=1, grid=(db,),
            in_specs=[pl.BlockSpec((1, rp, W_Q), lambda i, pt: (i, 0, 0)),
                      pl.BlockSpec((1, rp, LANE), lambda i, pt: (i, 0, 0)),
                      pl.BlockSpec((1, 2, LANE, LANE), lambda i, pt: (i, 0, 0, 0)),
                      pl.BlockSpec(memory_space=pl.ANY),
                      pl.BlockSpec((1, 1, 2 * LANE, wbs), lambda i, pt: (layer, i, 0, 0)),
                      pl.BlockSpec((1, rp, 2 * LANE), lambda i, pt: (i, 0, 0)),
                      pl.BlockSpec((1, rp, 2 * LANE), lambda i, pt: (i, 0, 0)),
                      pl.BlockSpec((2 * LANE, nk), lambda i, pt: (0, 0))],
            out_specs=pl.BlockSpec((1, rp, W_Q), lambda i, pt: (i, 0, 0)),
            scratch_shapes=[pltpu.VMEM((LANE, nk), F32), pltpu.VMEM((LANE, nk), F32),
                            pltpu.SemaphoreType.DMA(())]),
        out_shape=jax.ShapeDtypeStruct((db, rp, W_Q), F32),
        compiler_params=_cparams(("arbitrary",)),
        name="nsa_sample",
    )(page_table, qn, small, kcs, cn, wst, slcn, winn, eb)


def _logf_sample_kernel(pt_ref, cl_ref, utri_ref, sl_ref, o_ref, lbuf, sem, *, layer, n_pages):
    b = pl.program_id(0)

    def copy(p):
        return pltpu.make_async_copy(cl_ref.at[layer, pt_ref[b, p]], lbuf.at[pl.ds(p * H_FOX, H_FOX), :], sem)

    def start(p, c):
        copy(p).start()
        return c

    def wait(p, c):
        copy(p).wait()
        return c

    lax.fori_loop(0, n_pages, start, 0)
    lax.fori_loop(0, n_pages, wait, 0)

    def split3(x):
        hi = x.astype(BF16)
        rem = x - hi.astype(F32)
        mid = rem.astype(BF16)
        return hi, mid, (rem - mid.astype(F32)).astype(BF16)

    utri = utri_ref[...]
    w = jnp.zeros(lbuf.shape, F32)
    for part in split3(lbuf[...]):
        w = w + jnp.dot(part, utri, preferred_element_type=F32)
    tot = jnp.broadcast_to(w[:, LANE - 1:LANE], w.shape)
    carry = jnp.zeros(lbuf.shape, F32)
    sl = sl_ref[...]
    for part in split3(tot):
        carry = carry + jnp.dot(sl, part, preferred_element_type=F32)
    o_ref[0] = w + carry


def _logf_sample(page_table, cl, utri, slm, layer):
    db, n_pages = page_table.shape
    rows = n_pages * H_FOX
    return pl.pallas_call(
        functools.partial(_logf_sample_kernel, layer=layer, n_pages=n_pages),
        grid_spec=pltpu.PrefetchScalarGridSpec(
            num_scalar_prefetch=1, grid=(db,),
            in_specs=[pl.BlockSpec(memory_space=pl.ANY),
                      pl.BlockSpec((LANE, LANE), lambda i, pt: (0, 0)),
                      pl.BlockSpec((rows, rows), lambda i, pt: (0, 0))],
            out_specs=pl.BlockSpec((1, rows, LANE), lambda i, pt: (i, 0, 0)),
            scratch_shapes=[pltpu.VMEM((rows, LANE), F32), pltpu.SemaphoreType.DMA(())]),
        out_shape=jax.ShapeDtypeStruct((db, rows, LANE), F32),
        compiler_params=_cparams(("arbitrary",)),
        name="logf_sample",
    )(page_table, cl, utri, slm)


def _fox_sample_kernel(pt_ref, q_ref, c_ref, clast_ref, lfn_ref, kn_ref, vn_ref, place_ref, cf_ref, o_ref,
                       kbuf, vbuf, sem, *, layer, n_pages, dseq, rp):
    b = pl.program_id(0)
    pair = pl.program_id(1)
    half = H_FOX * HEAD_DIM

    def copies(p):
        page = pt_ref[b, p]
        win = pl.ds(pl.multiple_of(p * LANE, LANE), LANE)
        r0 = pl.multiple_of(pair * LANE, LANE)
        return (_page_copy(cf_ref, layer, page, r0, LANE, kbuf.at[:, win], sem),
                _page_copy(cf_ref, layer, page, half + r0, LANE, vbuf.at[:, win], sem))

    def start(p, c):
        for cp in copies(p):
            cp.start()
        return c

    def wait(p, c):
        for cp in copies(p):
            cp.wait()
        return c

    lax.fori_loop(0, n_pages, start, 0)
    qa = _qf_aug(q_ref[0], pair, rp).astype(BF16)
    r = 2 * rp
    nk = n_pages * LANE

    rows8 = lax.broadcasted_iota(jnp.int32, (rp, LANE), 0)
    lfn = lfn_ref[0]
    cn = jnp.broadcast_to(clast_ref[0], (rp, LANE))
    for k in range(dseq):
        cn = cn + jnp.where(rows8 >= k, lfn[k:k + 1, :], 0.0)
    hi = cn.astype(BF16)
    rem = cn - hi.astype(F32)
    mid = rem.astype(BF16)
    lo = (rem - mid.astype(F32)).astype(BF16)
    caug = (jnp.dot(hi, place_ref[0], preferred_element_type=F32) + jnp.dot(mid, place_ref[1], preferred_element_type=F32)
            + jnp.dot(lo, place_ref[2], preferred_element_type=F32)).astype(BF16)
    ka_n = _pad_rows(jnp.concatenate([kn_ref[0], caug], axis=1), LANE)
    s_n = _nt(qa, ka_n)
    lane_r = lax.broadcasted_iota(jnp.int32, (r, LANE), 1)
    tok_r = jnp.minimum(lax.broadcasted_iota(jnp.int32, (r, LANE), 0) & (rp - 1), dseq - 1)
    s_n = jnp.where((lane_r < dseq) & (lane_r <= tok_r), s_n, -BIG)

    lax.fori_loop(0, n_pages, wait, 0)
    s_p = jnp.dot(qa[:, 0:LANE], kbuf[...].astype(BF16), preferred_element_type=F32)
    s_p = (s_p.reshape(2, rp, nk) - c_ref[0, 0][:, None, :]).reshape(r, nk)
    (e_p, e_n), den = _softmax2([s_p, s_n])
    o = (_nt(e_p.astype(BF16), vbuf[...].astype(BF16))
         + jnp.dot(e_n.astype(BF16), _pad_rows(vn_ref[0], LANE), preferred_element_type=F32)) / den
    o = o.reshape(2, rp, LANE)
    lane = lax.broadcasted_iota(jnp.int32, (rp, LANE), 1)
    o_ref[0] = jnp.where(lane < HEAD_DIM, o[0], o[1])


def _fox_sample(page_table, qf, cflat, clast, small, kvfb, place, cf, layer, dseq):
    db, n_pages = page_table.shape
    rp = qf.shape[1]
    npair = H_FOX // 2
    nk = n_pages * LANE
    kern = functools.partial(_fox_sample_kernel, layer=layer, n_pages=n_pages, dseq=dseq, rp=rp)
    return pl.pallas_call(
        kern,
        grid_spec=pltpu.PrefetchScalarGridSpec(
            num_scalar_prefetch=1, grid=(db, npair),
            in_specs=[pl.BlockSpec((1, rp, LANE), lambda i, p, pt: (i, 0, p)),
                      pl.BlockSpec((1, 1, 2, nk), lambda i, p, pt: (i, p, 0, 0)),
                      pl.BlockSpec((1, 1, LANE), lambda i, p, pt: (i, 0, 0)),
                      pl.BlockSpec((1, rp, LANE), lambda i, p, pt: (i, 0, 0)),
                      pl.BlockSpec((1, rp, LANE), lambda i, p, pt: (i, 0, p)),
                      pl.BlockSpec((1, rp, LANE), lambda i, p, pt: (i, 0, npair + p)),
                      pl.BlockSpec((3, LANE, LANE), lambda i, p, pt: (0, 0, 0)),
                      pl.BlockSpec(memory_space=pl.ANY)],
            out_specs=pl.BlockSpec((1, rp, LANE), lambda i, p, pt: (i, 0, p)),
            scratch_shapes=[pltpu.VMEM((LANE, nk), F32), pltpu.VMEM((LANE, nk), F32),
                            pltpu.SemaphoreType.DMA(())]),
        out_shape=jax.ShapeDtypeStruct((db, rp, W_QF), F32),
        compiler_params=_cparams(("arbitrary", "arbitrary")),
        name="fox_sample",
    )(page_table, qf, cflat, clast, small, kvfb, kvfb, place, cf)


def _perm_heads(x, axis):
    shp = x.shape
    x = x.reshape(shp[:axis] + (G_NSA, HPG) + shp[axis + 1:])
    x = jnp.swapaxes(x, axis, axis + 1)
    return x.reshape(shp)


def _prep_layer_weights(w_in, b_gate, b_forget, cmp_pe, w_cmp_k, w_cmp_v, g_group, w_out):
    d = w_in.shape[0]
    o = 0
    qn = w_in[:, o:o + W_Q]; o += W_Q
    kvn = w_in[:, o:o + W_KVN]; o += W_KVN
    gate = w_in[:, o:o + W_GATE]; o += W_GATE
    qf = w_in[:, o:o + W_QF]; o += W_QF
    kvf = w_in[:, o:o + W_KVF]; o += W_KVF
    fg = w_in[:, o:o + H_FOX]
    qn = _perm_heads(qn.reshape(d, H_NSA, HEAD_DIM), 1).reshape(d, W_Q)
    gate = _perm_heads(gate.reshape(d, 3, H_NSA), 2).reshape(d, W_GATE)
    small = jnp.concatenate([gate, fg, jnp.zeros((d, LANE - W_GATE - H_FOX), w_in.dtype)], axis=1)
    w_proj = jnp.concatenate([qn, kvn, qf, kvf, small], axis=1).astype(BF16)
    brow = jnp.concatenate([_perm_heads(b_gate.reshape(3, H_NSA), 1).reshape(W_GATE), b_forget,
                            jnp.zeros((LANE - W_GATE - H_FOX,), F32)]).reshape(1, LANE)
    w4 = jnp.stack([w_cmp_k, w_cmp_k, w_cmp_v, w_cmp_v])
    eye = jnp.eye(8, 4, dtype=F32)
    w_big = jnp.einsum("xc,clde->lxdce", eye, w4).reshape(NSA_BLOCK * 8 * HEAD_DIM, 4 * HEAD_DIM).astype(BF16)
    pe_row = jnp.einsum("x,ld->lxd", jnp.concatenate([jnp.ones((4,), F32), jnp.zeros((4,), F32)]), cmp_pe)
    pe_rows = jnp.zeros((8, NSA_BLOCK * 8 * HEAD_DIM), F32).at[0].set(pe_row.reshape(-1)).astype(BF16)
    gg = jnp.concatenate([_perm_heads(g_group[:W_Q].reshape(H_NSA, HEAD_DIM), 0).reshape(W_Q),
                          g_group[W_Q:]]).reshape(1, -1)
    w_o = jnp.concatenate([_perm_heads(w_out[:W_Q].reshape(H_NSA, HEAD_DIM, -1), 0).reshape(W_Q, -1),
                           w_out[W_Q:]], axis=0).astype(BF16)
    return w_proj, brow, w_big, pe_rows, gg, w_o


def _pos_aug(pos):
    n = pos.shape[0]
    z = jnp.zeros((n, LANE), F32)
    z = z.at[:, 0].set((pos // NSA_BLOCK).astype(F32)).at[:, 1].set((pos % NSA_BLOCK).astype(F32))
    return z.at[:, 2].set(1.0).at[:, 3].set(1.0).astype(BF16)


def _expand_t(nkeys, nblk_pad):
    kb = jnp.arange(nkeys)[:, None] // NSA_BLOCK
    return jnp.where(kb == jnp.arange(nblk_pad)[None, :], BIG, 0.0).astype(BF16)


def _fox_consts(tc):
    ltri = (jnp.arange(tc)[:, None] >= jnp.arange(tc)[None, :]).astype(BF16)
    place = np.zeros((3, LANE, LANE), np.float32)
    for i in range(3):
        for h in range(H_FOX):
            place[i, LOGF_LANE + h, 8 * (h // 2) + 2 * i + (h % 2)] = 1.0
    return ltri, jnp.asarray(place, BF16)


def _mixer_prompt(x, lw, consts, b, seq, tiles):
    w_proj, brow, w_big, pe_rows, gg, w_o, g_pre, g_post = lw
    m = b * seq
    qn, nsakv, winf, slcb, winb, qf, kvf, kvfb, small = _proj(x, g_pre, w_proj, brow, tiles["tm"])
    nblk = seq // NSA_BLOCK
    nb = b * nblk
    kvc = _compress(nsakv.reshape(nb, NSA_BLOCK * 512), pe_rows, w_big, min(256, nb), 4096).reshape(b, nblk, 256)
    if nblk < LANE:
        kvc = jnp.pad(kvc, ((0, 0), (0, LANE - nblk), (0, 0)))
    o_n = _nsa_prompt(qn.reshape(b, seq, W_Q), small.reshape(b, seq, LANE), kvc, consts["cpos"],
                      slcb.reshape(b, seq, 256), winb.reshape(b, seq, 256), consts["kpos"], consts["ebt"],
                      tiles["nsa_tq"], tiles["nsa_tk"])
    caug = _fox_prep(small.reshape(b, seq, LANE), consts["ltri"], consts["place"], tiles["tc"])
    o_f = _fox_prompt(qf.reshape(b, seq, W_QF), kvfb.reshape(b, seq, W_KVF), caug, tiles["fox_tq"])
    y = _mixout(o_n.reshape(m, W_Q), o_f.reshape(m, W_QF), x, gg, w_o, g_post, tiles["tm"])
    return y, nsakv, winf, kvf, small


RP = 8


def _sample_consts(n_pages, w_cmp_k, w_cmp_v, cmp_pe):
    w2 = jnp.stack([w_cmp_k, w_cmp_v])
    wcd = jnp.einsum("xy,clde->cdxlye", jnp.eye(2, dtype=F32), w2)
    wcd = wcd.reshape(2, HEAD_DIM, 2 * NSA_BLOCK, 2 * HEAD_DIM).astype(BF16)
    pet = jnp.tile(cmp_pe.T, (1, 2))
    return wcd, pet


def _expand_sample(n_pages):
    rowi = jnp.arange(2 * LANE)
    blk = 2 * (rowi % 64) + rowi // LANE
    ok = (rowi % 64) < n_pages
    kb = jnp.arange(n_pages * LANE) // NSA_BLOCK
    return jnp.where((blk[:, None] == kb[None, :]) & ok[:, None], BIG, 0.0).astype(BF16)


def _logf_consts(n_pages):
    utri = (jnp.arange(LANE)[:, None] <= jnp.arange(LANE)[None, :]).astype(BF16)
    ri = jnp.arange(n_pages * H_FOX)
    slm = ((ri[:, None] % H_FOX == ri[None, :] % H_FOX) & (ri[None, :] // H_FOX < ri[:, None] // H_FOX)).astype(BF16)
    return utri, slm


def _mixer_sample(x, lw, sc, views, page_table, layer, past, dseq):
    w_proj, brow, _, _, gg, w_o, g_pre, g_post = lw
    cn, cf, cl, wst = views
    db, n_pages = page_table.shape
    m = db * RP
    qn, nsakv, winf, slcb, winb, qf, kvf, kvfb, small = _proj(x, g_pre, w_proj, brow, m)
    kcs = _cmp_sample(page_table, cn, sc["pet"], sc["wcd"], layer)
    if n_pages < 64:
        kcs = jnp.pad(kcs.reshape(db, 2, G_NSA, n_pages, LANE), ((0, 0),) * 3 + ((0, 64 - n_pages), (0, 0)))
        kcs = kcs.reshape(db, 2, LANE, LANE)
    o_n = _nsa_sample(page_table, qn.reshape(db, RP, W_Q), small.reshape(db, RP, LANE), kcs, cn, wst,
                      slcb.reshape(db, RP, 256), winb.reshape(db, RP, 256), sc["eb"], layer, past, dseq)
    c_rows = _logf_sample(page_table, cl, sc["utri"], sc["slm"], layer)
    cflat = c_rows.reshape(db, n_pages, H_FOX // 2, 2, LANE).transpose(0, 2, 3, 1, 4)
    cflat = cflat.reshape(db, H_FOX // 2, 2, n_pages * LANE)
    clast = jnp.pad(cflat[:, :, :, -1].reshape(db, 1, H_FOX), ((0, 0), (0, 0), (LOGF_LANE, LANE - LOGF_LANE - H_FOX)))
    o_f = _fox_sample(page_table, qf.reshape(db, RP, W_QF), cflat, clast, small.reshape(db, RP, LANE),
                      kvfb.reshape(db, RP, W_KVF), sc["place"], cf, layer, dseq)
    y = _mixout(o_n.reshape(m, W_Q), o_f.reshape(m, W_QF), x, gg, w_o, g_post, m)
    return y, nsakv, winf, kvf, small


def _tiles(seq):
    return dict(tm=min(512, seq), nsa_tq=128, nsa_tk=min(512, seq), tc=min(256, seq), fox_tq=min(512, seq),
                ffn_tm=min(512, seq))


def kernel(x_prompt, x_sample, cache_nsa_kv, state_nsa_win, cache_fox_kv, cache_fox_logf, state_conv, page_table,
           g_attn_pre, g_attn_post, g_ffn_pre, g_ffn_post, w_mix_in, b_gate, b_forget, cmp_pe, w_cmp_k, w_cmp_v,
           g_group, w_mix_out, w_ffn_in, conv_w, conv_b, w_ffn_out):
    depth = w_mix_in.shape[0]
    b, seq, d = x_prompt.shape
    db, dseq, _ = x_sample.shape
    dff = conv_b.shape[1]
    wb = min(NSA_WINDOW, seq)
    tiles = _tiles(seq)
    tf = dff // 2
    ltri, place = _fox_consts(tiles["tc"])
    consts = dict(cpos=_pos_aug(jnp.arange(LANE) * NSA_BLOCK + (NSA_BLOCK - 1)), kpos=_pos_aug(jnp.arange(seq)),
                  ebt=_expand_t(seq, LANE), ltri=ltri, place=place)

    n_pages, page = page_table.shape[1], cache_nsa_kv.shape[2]
    past = n_pages * page
    wbs = state_nsa_win.shape[2]
    assert page == LANE and past % NSA_BLOCK == 0 and dseq <= min(RP, NSA_BLOCK) and n_pages <= 64
    cn = jnp.transpose(cache_nsa_kv, (0, 1, 3, 4, 5, 2)).reshape(depth, -1, 4 * G_NSA * HEAD_DIM, page)
    cf = jnp.transpose(cache_fox_kv, (0, 1, 3, 4, 5, 2)).reshape(depth, -1, W_KVF, page)
    cl = jnp.transpose(cache_fox_logf, (0, 1, 3, 2))
    wst = jnp.transpose(state_nsa_win, (0, 1, 3, 4, 5, 2)).reshape(depth, db, 2 * G_NSA * HEAD_DIM, wbs)
    views = (cn, cf, cl, wst)
    utri, slm = _logf_consts(n_pages)
    eb_s = _expand_sample(n_pages)
    xs = jnp.pad(x_sample, ((0, 0), (0, RP - dseq), (0, 0))).reshape(db * RP, d)
    s_nsa, s_win, s_fox, s_logf, s_conv = [], [], [], [], []

    xp = x_prompt.reshape(b * seq, d)
    p_nsa, p_win, p_fox, p_logf, p_conv = [], [], [], [], []
    for l in range(depth):
        w_proj, brow, w_big, pe_rows, gg, w_o = _prep_layer_weights(
            w_mix_in[l], b_gate[l], b_forget[l], cmp_pe[l], w_cmp_k[l], w_cmp_v[l], g_group[l], w_mix_out[l])
        lw = (w_proj, brow, w_big, pe_rows, gg, w_o, g_attn_pre[l].reshape(1, d), g_attn_post[l].reshape(1, d))
        wa = w_ffn_in[l, :, :dff].astype(BF16)
        wbb = w_ffn_in[l, :, dff:].astype(BF16)
        wo = w_ffn_out[l].astype(BF16)
        ffn_w = (g_ffn_pre[l].reshape(1, d), wa, wbb, conv_w[l], conv_b[l].reshape(1, dff), wo,
                 g_ffn_post[l].reshape(1, d))

        xp, nsakv, winf, kvf, small = _mixer_prompt(xp, lw, consts, b, seq, tiles)
        xp, a_tail = _ffn(xp, None, *ffn_w, tm=tiles["ffn_tm"], tf=tf, seq=seq, tail=8)
        p_nsa.append(nsakv.reshape(b, seq, 4, G_NSA, HEAD_DIM))
        p_win.append(winf.reshape(b, seq, 2, G_NSA, HEAD_DIM)[:, seq - wb:])
        p_fox.append(kvf.reshape(b, seq, 2, H_FOX, HEAD_DIM))
        p_logf.append(small.reshape(b, seq, LANE)[:, :, LOGF_LANE:LOGF_LANE + H_FOX])
        nbk = seq // tiles["ffn_tm"]
        p_conv.append(a_tail.reshape(b, nbk, 8, dff)[:, nbk - 1, 8 - (CONV_W - 1):])

        wcd, pet = _sample_consts(n_pages, w_cmp_k[l], w_cmp_v[l], cmp_pe[l])
        sc = dict(wcd=wcd, pet=pet, eb=eb_s, utri=utri, slm=slm, place=place)
        xs, nsakv, winf, kvf, small = _mixer_sample(xs, lw, sc, views, page_table, l, past, dseq)
        st = jnp.zeros((db + 1, RP, dff), F32).at[:db, RP - (CONV_W - 1):].set(state_conv[l])
        xs, a_full = _ffn(xs, st.reshape((db + 1) * RP, dff), *ffn_w, tm=db * RP, tf=tf, seq=RP, tail=db * RP)
        s_nsa.append(nsakv.reshape(db, RP, 4, G_NSA, HEAD_DIM)[:, :dseq])
        new_win = winf.reshape(db, RP, 2, G_NSA, HEAD_DIM)[:, :dseq]
        s_win.append(jnp.concatenate([state_nsa_win[l], new_win], axis=1)[:, dseq:])
        s_fox.append(kvf.reshape(db, RP, 2, H_FOX, HEAD_DIM)[:, :dseq])
        s_logf.append(small.reshape(db, RP, LANE)[:, :dseq, LOGF_LANE:LOGF_LANE + H_FOX])
        a_rows = jnp.concatenate([state_conv[l], a_full.reshape(db, RP, dff)[:, :dseq]], axis=1)
        s_conv.append(a_rows[:, dseq:])

    return (xp.reshape(b, seq, d), xs.reshape(db, RP, d)[:, :dseq],
            jnp.stack(p_nsa), jnp.stack(s_nsa),
            jnp.stack(p_win), jnp.stack(s_win),
            jnp.stack(p_fox), jnp.stack(s_fox),
            jnp.stack(p_logf), jnp.stack(s_logf),
            jnp.stack(p_conv), jnp.stack(s_conv))
```

```python
import functools

import jax
import jax.numpy as jnp
import numpy as np
from jax import lax
from jax.experimental import pallas as pl
from jax.experimental.pallas import tpu as pltpu

F32 = jnp.float32
BF16 = jnp.bfloat16

HEAD_DIM = 64
G_NSA = 2
HPG = 4
H_NSA = G_NSA * HPG
H_FOX = 8
NSA_BLOCK = 64
NSA_TOPK = 16
NSA_WINDOW = 512
CONV_W = 3
RMS_EPS = 1e-6
FORCE_SCORE = 1e4
SCALE = HEAD_DIM ** -0.5

LANE = 128
VMEM_LIMIT = 56 * 1024 * 1024
NEG = -1e30
BIG = 2.0 ** 100
M_INIT = -1e38

W_Q = H_NSA * HEAD_DIM
W_KVN = 6 * G_NSA * HEAD_DIM
W_GATE = 3 * H_NSA
W_QF = H_FOX * HEAD_DIM
W_KVF = 2 * H_FOX * HEAD_DIM
C_QN, C_KVN, C_QF, C_KVF, C_SMALL = 0, 512, 1280, 1792, 2816
N_PROJ = C_SMALL + LANE
LOGF_LANE = W_GATE


def _cparams(sem):
    return pltpu.CompilerParams(dimension_semantics=sem, vmem_limit_bytes=VMEM_LIMIT)


def _nt(a, b):
    return lax.dot_general(a, b, (((1,), (1,)), ((), ())), preferred_element_type=F32)


def _rms(x, g):
    return x * lax.rsqrt(jnp.mean(x * x, axis=-1, keepdims=True) + RMS_EPS) * g


def _proj_kernel(x_ref, g_ref, w_ref, b_ref, qn_ref, nsakv_ref, winf_ref, cmpb_ref, slcb_ref, winb_ref,
                 qf_ref, kvf_ref, kvfb_ref, small_ref, *, feature_major):
    h = _rms(x_ref[...], g_ref[...]).astype(BF16)

    def mm(c0, c1):
        return jnp.dot(h, w_ref[:, c0:c1], preferred_element_type=F32)

    def put(ref, u):
        if feature_major:
            ref[0] = u.T
        else:
            ref[...] = u

    qn_ref[...] = (mm(C_QN, C_QN + W_Q) * SCALE).astype(BF16)
    u = mm(C_KVN, C_KVN + 512)
    put(nsakv_ref, u)
    cmpb_ref[...] = u[:, 0:256].astype(BF16)
    slcb_ref[...] = u[:, 256:512].astype(BF16)
    u = mm(C_KVN + 512, C_KVN + 768)
    put(winf_ref, u)
    winb_ref[...] = u.astype(BF16)
    qf_ref[...] = (mm(C_QF, C_QF + W_QF) * SCALE).astype(BF16)
    u = mm(C_KVF, C_KVF + W_KVF)
    put(kvf_ref, u)
    kvfb_ref[...] = u.astype(BF16)
    z = mm(C_SMALL, N_PROJ) + b_ref[...]
    lane = lax.broadcasted_iota(jnp.int32, z.shape, 1)
    log_sig = jnp.minimum(z, 0.0) - jnp.log(1.0 + jnp.exp(-jnp.abs(z)))
    small_ref[...] = jnp.where(lane < W_GATE, jax.nn.sigmoid(z), log_sig)


def _proj(x, g, w, brow, tm, seq=None):
    m, d = x.shape
    fm = seq is not None
    widths = [(W_Q, BF16, False), (512, F32, fm), (256, F32, fm), (256, BF16, False), (256, BF16, False),
              (256, BF16, False), (W_QF, BF16, False), (W_KVF, F32, fm), (W_KVF, BF16, False), (LANE, F32, False)]
    nblk = (seq // tm) if fm else 1

    def spec(wd, t):
        if t:
            return pl.BlockSpec((1, wd, tm), lambda i: (i // nblk, 0, i % nblk))
        return pl.BlockSpec((tm, wd), lambda i: (i, 0))

    def shape(wd, dt, t):
        return jax.ShapeDtypeStruct((m // seq, wd, seq) if t else (m, wd), dt)

    return pl.pallas_call(
        functools.partial(_proj_kernel, feature_major=fm),
        grid=(m // tm,),
        in_specs=[pl.BlockSpec((tm, d), lambda i: (i, 0)),
                  pl.BlockSpec((1, d), lambda i: (0, 0)),
                  pl.BlockSpec((d, N_PROJ), lambda i: (0, 0)),
                  pl.BlockSpec((1, LANE), lambda i: (0, 0))],
        out_specs=[spec(wd, t) for wd, _, t in widths],
        out_shape=[shape(wd, dt, t) for wd, dt, t in widths],
        compiler_params=_cparams(("parallel",)),
        name="proj",
    )(x, g, w, brow)


def _compress_kernel(x_ref, pe_ref, w_ref, o_ref, acc_ref, accpe_ref):
    k = pl.program_id(1)

    @pl.when(k == 0)
    def _():
        acc_ref[...] = jnp.zeros_like(acc_ref)
        accpe_ref[...] = jnp.zeros_like(accpe_ref)

    w = w_ref[...]
    acc_ref[...] += jnp.dot(x_ref[...].astype(BF16), w, preferred_element_type=F32)
    accpe_ref[...] += jnp.dot(pe_ref[...], w, preferred_element_type=F32)

    @pl.when(k == pl.num_programs(1) - 1)
    def _():
        o_ref[...] = (acc_ref[...] + accpe_ref[0:1, :]).astype(BF16)


def _compress(x, pe_rows, w_big, tmb, tkc):
    nb, kk = x.shape
    return pl.pallas_call(
        _compress_kernel,
        grid=(nb // tmb, kk // tkc),
        in_specs=[pl.BlockSpec((tmb, tkc), lambda i, k: (i, k)),
                  pl.BlockSpec((8, tkc), lambda i, k: (0, k)),
                  pl.BlockSpec((tkc, 256), lambda i, k: (k, 0))],
        out_specs=pl.BlockSpec((tmb, 256), lambda i, k: (i, 0)),
        out_shape=jax.ShapeDtypeStruct((nb, 256), BF16),
        scratch_shapes=[pltpu.VMEM((tmb, 256), F32), pltpu.VMEM((8, 256), F32)],
        compiler_params=_cparams(("parallel", "arbitrary")),
        name="compress",
    )(x, pe_rows, w_big)


_SKIP = "skip"


def _flash_tile(qa_ref, ka, va, m_ref, acc_ref, chunk, bias_fn=None):
    r = qa_ref.shape[0]
    tk = ka.shape[0]
    for c0 in range(0, r, chunk):
        rows = slice(c0, c0 + chunk)
        bias = None if bias_fn is None else bias_fn(c0)
        if bias is _SKIP:
            continue
        s = _nt(qa_ref[rows, :], ka)
        if bias is not None:
            s = s + bias
        m_prev = m_ref[rows, :]
        m_new = jnp.maximum(m_prev, jnp.max(s, axis=-1, keepdims=True))
        alpha = jnp.exp(m_prev - m_new)
        p = jnp.exp(s - jnp.tile(m_new, (1, tk // LANE)))
        acc_ref[rows, :] = (jnp.tile(alpha, (1, 2)) * acc_ref[rows, :]
                            + jnp.dot(p.astype(BF16), va, preferred_element_type=F32))
        m_ref[rows, :] = m_new


def _with_ones(v):
    return jnp.concatenate([v, jnp.ones_like(v)], axis=1)


def _masked_softmax(s, mask):
    s = jnp.where(mask, s, NEG)
    m = jnp.max(s, axis=-1, keepdims=True)
    e = jnp.where(mask, jnp.exp(s - m), 0.0)
    d = jnp.sum(e, axis=-1, keepdims=True)
    return e / jnp.maximum(d, 1e-30)


def _topk_select(score, blkid, k):
    sel = jnp.zeros_like(score)
    big = jnp.int32(1 << 30)
    for _ in range(k):
        mx = jnp.max(score, axis=-1, keepdims=True)
        idx = jnp.min(jnp.where(score == mx, blkid, big), axis=-1, keepdims=True)
        hit = blkid == idx
        sel = jnp.where(hit, jnp.where(mx >= 0.0, 1.0, 0.0), sel)
        score = jnp.where(hit, -3e38, score)
    return sel


def _topk_select_lanes(score, valid, k):
    st = score.T
    ok = valid.astype(F32).T > 0.0
    blk = lax.broadcasted_iota(jnp.int32, st.shape, 0)
    sel = jnp.zeros_like(st)
    for _ in range(k):
        mx = jnp.max(st, axis=0, keepdims=True)
        first = jnp.min(jnp.where(st == mx, blk, st.shape[0]), axis=0, keepdims=True)
        hit = blk == first
        sel = jnp.where(hit & ok, 1.0, sel)
        st = jnp.where(hit, -3e38, st)
    return sel.T


def _slope(g, j):
    return 2.0 ** (-(g * HPG + j + 1))


def _build_q_aug(q_ref, qa_ref, qpos, tq):
    lane = lax.broadcasted_iota(jnp.int32, (tq, LANE), 1)
    qblk = (qpos >> 6).astype(F32)
    qin = (qpos & 63).astype(F32)
    for g in range(G_NSA):
        for j in range(HPG):
            sl = _slope(g, j)
            qb = q_ref[0, :, j * LANE:(j + 1) * LANE]
            qm = jnp.where((lane >> 6) == g, qb, jnp.zeros_like(qb))
            aug = jnp.where(lane == 0, sl * NSA_BLOCK,
                            jnp.where(lane == 1, sl,
                                      jnp.where(lane == 2, -sl * NSA_BLOCK * qblk,
                                                jnp.where(lane == 3, -sl * qin, 0.0))))
            r0 = (g * HPG + j) * tq
            qa_ref[r0:r0 + tq, :] = jnp.concatenate([qm, aug.astype(BF16)], axis=1)


def _nsa_prompt_kernel(q_ref, gate_ref, kvc_ref, cpos_ref, slc_ref, win_ref, kpos_ref, ebt_ref, o_ref,
                       qa_ref, selm_ref, m_ref, acc_ref, og_ref, *, tq, tk, seq, nblk, chunk):
    qi = pl.program_id(1)
    q0 = qi * tq
    nh = H_NSA
    row = lax.broadcasted_iota(jnp.int32, (tq, LANE), 0)
    lane = lax.broadcasted_iota(jnp.int32, (tq, LANE), 1)
    qpos = q0 + row
    _build_q_aug(q_ref, qa_ref, qpos, tq)

    def add_gated(o, branch, first):
        for g in range(G_NSA):
            for j in range(HPG):
                hh = g * HPG + j
                c = branch * H_NSA + 2 * j + g
                rows = slice(hh * tq, (hh + 1) * tq)
                val = gate_ref[0, :, c:c + 1] * o[rows, :]
                og_ref[rows, :] = val if first else og_ref[rows, :] + val

    kca = jnp.concatenate([kvc_ref[0, :, 0:LANE], cpos_ref[...]], axis=1)
    s = _nt(qa_ref[...], kca).reshape(nh, tq, LANE)
    vis = (lane * NSA_BLOCK + (NSA_BLOCK - 1) <= qpos) & (lane < nblk)
    p = _masked_softmax(s, vis[None])
    add_gated(jnp.dot(p.reshape(nh * tq, LANE).astype(BF16), kvc_ref[0, :, LANE:2 * LANE],
                      preferred_element_type=F32), 0, True)

    cur = qpos >> 6
    forced = (lane == 0) | (lane == cur) | (lane == cur - 1)
    anyblk = jnp.zeros((1, LANE), F32)
    for g in range(G_NSA):
        imp = p[g * HPG] + p[g * HPG + 1] + p[g * HPG + 2] + p[g * HPG + 3]
        score = jnp.where(lane <= cur, jnp.where(forced, FORCE_SCORE, imp), -1.0)
        sel = _topk_select_lanes(score, lane <= cur, NSA_TOPK)
        selm_ref[g] = (sel - 1.0).astype(BF16)
        anyblk = jnp.maximum(anyblk, jnp.max(sel, axis=0, keepdims=True))

    def finish(branch):
        acc = acc_ref[...]
        add_gated(acc[:, 0:LANE] / acc[:, LANE:2 * LANE], branch, False)

    m_ref[...] = jnp.full_like(m_ref, M_INIT)
    acc_ref[...] = jnp.zeros_like(acc_ref)
    kt_last = (q0 + tq - 1) // tk

    def tile(kt, causal):
        k0 = pl.multiple_of(kt * tk, tk)
        ka = jnp.concatenate([slc_ref[0, pl.ds(k0, tk), 0:LANE], kpos_ref[pl.ds(k0, tk), :]], axis=1)
        va = _with_ones(slc_ref[0, pl.ds(k0, tk), LANE:2 * LANE])
        ebt = ebt_ref[pl.ds(k0, tk), :]
        mbs = []
        for g in range(G_NSA):
            mb = _nt(selm_ref[g], ebt)
            if causal:
                kp = k0 + lax.broadcasted_iota(jnp.int32, (tq, tk), 1)
                qp = q0 + lax.broadcasted_iota(jnp.int32, (tq, tk), 0)
                mb = jnp.where(kp <= qp, mb, -BIG)
            mbs.append(mb)

        def bias_fn(c0):
            return jnp.tile(mbs[c0 // (HPG * tq)], (chunk // tq, 1))

        _flash_tile(qa_ref, ka, va, m_ref, acc_ref, chunk, bias_fn)

    bpt = tk // NSA_BLOCK
    blk1 = lax.broadcasted_iota(jnp.int32, (1, LANE), 1)

    def body(kt, carry):
        in_tile = (blk1 >= kt * bpt) & (blk1 < (kt + 1) * bpt)
        wanted = jnp.max(jnp.where(in_tile, anyblk, 0.0))

        @pl.when(wanted > 0.0)
        def _():
            tile(kt, False)

        return carry

    lax.fori_loop(0, kt_last, body, 0)
    tile(kt_last, True)
    finish(1)

    wk = NSA_WINDOW + tq
    kstart = pl.multiple_of(jnp.clip(q0 - NSA_WINDOW, 0, seq - wk), 8)
    ka = jnp.concatenate([win_ref[0, pl.ds(kstart, wk), 0:LANE], kpos_ref[pl.ds(kstart, wk), :]], axis=1)
    va = _with_ones(win_ref[0, pl.ds(kstart, wk), LANE:2 * LANE])
    dist = (q0 - kstart) + (lax.broadcasted_iota(jnp.int32, (tq, wk), 0)
                            - lax.broadcasted_iota(jnp.int32, (tq, wk), 1))
    wmask = jnp.where((dist >= 0) & (dist < NSA_WINDOW), 0.0, -BIG)
    m_ref[...] = jnp.full_like(m_ref, M_INIT)
    acc_ref[...] = jnp.zeros_like(acc_ref)
    _flash_tile(qa_ref, ka, va, m_ref, acc_ref, chunk, lambda c0: jnp.tile(wmask, (chunk // tq, 1)))
    finish(2)

    for j in range(HPG):
        lo = og_ref[j * tq:(j + 1) * tq, :]
        hi = og_ref[(HPG + j) * tq:(HPG + j + 1) * tq, :]
        o_ref[0, :, j * LANE:(j + 1) * LANE] = jnp.where(lane < HEAD_DIM, lo, hi)


def _nsa_prompt(qn, small, kvc, cpos, slcb, winb, kpos, ebt, tq, tk, chunk):
    b, seq, _ = qn.shape
    nblk = seq // NSA_BLOCK
    kern = functools.partial(_nsa_prompt_kernel, tq=tq, tk=tk, seq=seq, nblk=nblk, chunk=chunk)
    rows = H_NSA * tq
    return pl.pallas_call(
        kern,
        grid=(b, seq // tq),
        in_specs=[pl.BlockSpec((1, tq, W_Q), lambda i, j: (i, j, 0)),
                  pl.BlockSpec((1, tq, LANE), lambda i, j: (i, j, 0)),
                  pl.BlockSpec((1, LANE, 256), lambda i, j: (i, 0, 0)),
                  pl.BlockSpec((LANE, LANE), lambda i, j: (0, 0)),
                  pl.BlockSpec((1, seq, 256), lambda i, j: (i, 0, 0)),
                  pl.BlockSpec((1, seq, 256), lambda i, j: (i, 0, 0)),
                  pl.BlockSpec((seq, LANE), lambda i, j: (0, 0)),
                  pl.BlockSpec((seq, LANE), lambda i, j: (0, 0))],
        out_specs=pl.BlockSpec((1, tq, W_Q), lambda i, j: (i, j, 0)),
        out_shape=jax.ShapeDtypeStruct((b, seq, W_Q), F32),
        scratch_shapes=[pltpu.VMEM((rows, 2 * LANE), BF16),
                        pltpu.VMEM((G_NSA, tq, LANE), BF16),
                        pltpu.VMEM((rows, LANE), F32),
                        pltpu.VMEM((rows, 2 * LANE), F32),
                        pltpu.VMEM((rows, LANE), F32)],
        compiler_params=_cparams(("parallel", "parallel")),
        name="nsa_prompt",
    )(qn, small, kvc, cpos, slcb, winb, kpos, ebt)


def _fox_prep_kernel(lf_ref, ltri_ref, place_ref, o_ref, carry_ref):
    t = pl.program_id(1)

    @pl.when(t == 0)
    def _():
        carry_ref[...] = jnp.zeros_like(carry_ref)

    def split3(x):
        hi = x.astype(BF16)
        r = x - hi.astype(F32)
        mid = r.astype(BF16)
        return hi, mid, (r - mid.astype(F32)).astype(BF16)

    ltri = ltri_ref[...]
    c = carry_ref[...]
    for part in split3(lf_ref[0]):
        c = c + jnp.dot(ltri, part, preferred_element_type=F32)
    tc = c.shape[0]
    carry_ref[...] = c[tc - 1:tc, :]
    out = jnp.zeros((tc, LANE), F32)
    for i, part in enumerate(split3(c)):
        out = out + jnp.dot(part, place_ref[i], preferred_element_type=F32)
    o_ref[0] = out.astype(BF16)


def _fox_prep(lf, ltri, place, tc):
    b, n, _ = lf.shape
    return pl.pallas_call(
        _fox_prep_kernel,
        grid=(b, n // tc),
        in_specs=[pl.BlockSpec((1, tc, LANE), lambda i, t: (i, t, 0)),
                  pl.BlockSpec((tc, tc), lambda i, t: (0, 0)),
                  pl.BlockSpec((3, LANE, LANE), lambda i, t: (0, 0, 0))],
        out_specs=pl.BlockSpec((1, tc, LANE), lambda i, t: (i, t, 0)),
        out_shape=jax.ShapeDtypeStruct((b, n, LANE), BF16),
        scratch_shapes=[pltpu.VMEM((1, LANE), F32)],
        compiler_params=_cparams(("parallel", "arbitrary")),
        name="fox_prep",
    )(lf, ltri, place)


def _qf_aug(q, pair, tq):
    lane = lax.broadcasted_iota(jnp.int32, (tq, LANE), 1)
    q = q.astype(F32)
    parts = []
    for e in range(2):
        qm = jnp.where((lane >> 6) == e, q, 0.0)
        d = lane - (8 * pair + e)
        aug = jnp.where(d == 0, -1.0, jnp.where(d == 2, -1.0, jnp.where(d == 4, -1.0, 0.0)))
        parts.append(jnp.concatenate([qm, aug], axis=1))
    return jnp.concatenate(parts, axis=0)


def _fox_prompt_kernel(q_ref, k_ref, v_ref, ca_ref, o_ref, qa_ref, m_ref, acc_ref, *, tq, tk, chunk):
    pair = pl.program_id(1)
    qi = pl.program_id(2)
    qa_ref[...] = _qf_aug(q_ref[0], pair, tq).astype(BF16)
    m_ref[...] = jnp.full_like(m_ref, M_INIT)
    acc_ref[...] = jnp.zeros_like(acc_ref)
    ndiag = tq // tk

    def tile(kt, diag):
        k0 = pl.multiple_of(kt * tk, tk)
        ka = jnp.concatenate([k_ref[0, pl.ds(k0, tk), :], ca_ref[0, pl.ds(k0, tk), :]], axis=1)
        va = _with_ones(v_ref[0, pl.ds(k0, tk), :])
        bias_fn = None
        if diag is not None:
            def bias_fn(c0):
                r0 = c0 % tq
                if r0 + chunk - 1 < diag * tk:
                    return _SKIP
                if r0 >= diag * tk + tk - 1:
                    return None
                qrow = r0 + lax.broadcasted_iota(jnp.int32, (chunk, tk), 0)
                kcol = diag * tk + lax.broadcasted_iota(jnp.int32, (chunk, tk), 1)
                return jnp.where(kcol <= qrow, 0.0, -BIG)
        _flash_tile(qa_ref, ka, va, m_ref, acc_ref, chunk, bias_fn)

    def body(kt, carry):
        tile(kt, None)
        return carry

    lax.fori_loop(0, qi * ndiag, body, 0)
    for dg in range(ndiag):
        tile(qi * ndiag + dg, dg)
    acc = acc_ref[...]
    o = (acc[:, 0:LANE] / acc[:, LANE:2 * LANE]).reshape(2, tq, LANE)
    lane = lax.broadcasted_iota(jnp.int32, (tq, LANE), 1)
    o_ref[0] = jnp.where(lane < HEAD_DIM, o[0], o[1])


def _fox_prompt(qf, kvfb, caug, tq, tk, chunk):
    b, seq, _ = qf.shape
    npair = H_FOX // 2
    return pl.pallas_call(
        functools.partial(_fox_prompt_kernel, tq=tq, tk=tk, chunk=min(chunk, tq)),
        grid=(b, npair, seq // tq),
        in_specs=[pl.BlockSpec((1, tq, LANE), lambda i, p, j: (i, j, p)),
                  pl.BlockSpec((1, seq, LANE), lambda i, p, j: (i, 0, p)),
                  pl.BlockSpec((1, seq, LANE), lambda i, p, j: (i, 0, npair + p)),
                  pl.BlockSpec((1, seq, LANE), lambda i, p, j: (i, 0, 0))],
        out_specs=pl.BlockSpec((1, tq, LANE), lambda i, p, j: (i, j, p)),
        out_shape=jax.ShapeDtypeStruct((b, seq, W_QF), F32),
        scratch_shapes=[pltpu.VMEM((2 * tq, 2 * LANE), BF16),
                        pltpu.VMEM((2 * tq, LANE), F32),
                        pltpu.VMEM((2 * tq, 2 * LANE), F32)],
        compiler_params=_cparams(("parallel", "parallel", "parallel")),
        name="fox_prompt",
    )(qf, kvfb, kvfb, caug)


def _mixout_kernel(on_ref, of_ref, x_ref, gg_ref, w_ref, gp_ref, o_ref):
    gg = gg_ref[...]
    half = on_ref.shape[1]
    yn = _rms(on_ref[...], gg[:, :half])
    yf = _rms(of_ref[...], gg[:, half:])
    y = jnp.dot(jnp.concatenate([yn, yf], axis=1).astype(BF16), w_ref[...], preferred_element_type=F32)
    o_ref[...] = x_ref[...] + _rms(y, gp_ref[...])


def _mixout(on, of, x, gg, w, gp, tm):
    m, d = x.shape
    half = on.shape[1]
    return pl.pallas_call(
        _mixout_kernel,
        grid=(m // tm,),
        in_specs=[pl.BlockSpec((tm, half), lambda i: (i, 0)),
                  pl.BlockSpec((tm, half), lambda i: (i, 0)),
                  pl.BlockSpec((tm, d), lambda i: (i, 0)),
                  pl.BlockSpec((1, 2 * half), lambda i: (0, 0)),
                  pl.BlockSpec((2 * half, d), lambda i: (0, 0)),
                  pl.BlockSpec((1, d), lambda i: (0, 0))],
        out_specs=pl.BlockSpec((tm, d), lambda i: (i, 0)),
        out_shape=jax.ShapeDtypeStruct((m, d), F32),
        compiler_params=_cparams(("parallel",)),
        name="mixout",
    )(on, of, x, gg, w, gp)


def _gelu_tanh(x):
    return 0.5 * x * (1.0 + jnp.tanh(np.sqrt(2.0 / np.pi) * (x + 0.044715 * (x * x * x))))


def _ffn_kernel(*refs, tm, seq, has_state, tail):
    if has_state:
        (x_ref, xh_ref, st_ref, g_ref, wa_ref, wb_ref, cw_ref, cb_ref, wo_ref, gp_ref,
         o_ref, a_ref, h_ref, acc_ref) = refs
    else:
        (x_ref, xh_ref, g_ref, wa_ref, wb_ref, cw_ref, cb_ref, wo_ref, gp_ref,
         o_ref, a_ref, h_ref, acc_ref) = refs
    i = pl.program_id(0)
    j = pl.program_id(1)

    @pl.when(j == 0)
    def _():
        g = g_ref[...]
        h_ref[0:8, :] = _rms(xh_ref[...], g).astype(BF16)
        h_ref[8:, :] = _rms(x_ref[...], g).astype(BF16)
        acc_ref[...] = jnp.zeros_like(acc_ref)

    a_ext = jnp.dot(h_ref[...], wa_ref[...], preferred_element_type=F32)
    rows = lax.broadcasted_iota(jnp.int32, a_ext.shape, 0)
    if has_state:
        a_ext = jnp.where((rows & 7) >= 6, st_ref[...], a_ext)
    else:
        keep = jnp.logical_or(rows >= 8, (i * tm) % seq != 0)
        a_ext = jnp.where(keep, a_ext, 0.0)
    a = a_ext[8:, :]
    a1 = pltpu.roll(a_ext, 1, 0)[8:, :]
    a2 = pltpu.roll(a_ext, 2, 0)[8:, :]
    cw = cw_ref[...]
    ac = cb_ref[...] + cw[0:1, :] * a2 + cw[1:2, :] * a1 + cw[2:3, :] * a
    b = jnp.dot(h_ref[8:, :], wb_ref[...], preferred_element_type=F32)
    acc_ref[...] += jnp.dot((_gelu_tanh(ac) * b).astype(BF16), wo_ref[...], preferred_element_type=F32)
    a_ref[0] = a[tm - tail:, :]

    @pl.when(j == pl.num_programs(1) - 1)
    def _():
        o_ref[...] = x_ref[...] + _rms(acc_ref[...], gp_ref[...])


def _ffn(x, state_ext, g, wa, wb, cw, cb, wo, gp, tm, tf, seq, tail):
    m, d = x.shape
    dff = wa.shape[1]
    has_state = state_ext is not None
    nb = m // tm
    halo = lambda i, j: (jnp.maximum(i * (tm // 8) - 1, 0), 0)
    in_specs = [pl.BlockSpec((tm, d), lambda i, j: (i, 0)), pl.BlockSpec((8, d), halo)]
    args = [x, x]
    if has_state:
        in_specs.append(pl.BlockSpec((tm + 8, tf), lambda i, j: (0, j)))
        args.append(state_ext)
    in_specs += [pl.BlockSpec((1, d), lambda i, j: (0, 0)),
                 pl.BlockSpec((d, tf), lambda i, j: (0, j)),
                 pl.BlockSpec((d, tf), lambda i, j: (0, j)),
                 pl.BlockSpec((CONV_W, tf), lambda i, j: (0, j)),
                 pl.BlockSpec((1, tf), lambda i, j: (0, j)),
                 pl.BlockSpec((tf, d), lambda i, j: (j, 0)),
                 pl.BlockSpec((1, d), lambda i, j: (0, 0))]
    args += [g, wa, wb, cw, cb, wo, gp]
    return pl.pallas_call(
        functools.partial(_ffn_kernel, tm=tm, seq=seq, has_state=has_state, tail=tail),
        grid=(nb, dff // tf),
        in_specs=in_specs,
        out_specs=[pl.BlockSpec((tm, d), lambda i, j: (i, 0)),
                   pl.BlockSpec((1, tail, tf), lambda i, j: (i, 0, j))],
        out_shape=[jax.ShapeDtypeStruct((m, d), F32), jax.ShapeDtypeStruct((nb, tail, dff), F32)],
        scratch_shapes=[pltpu.VMEM((tm + 8, d), BF16), pltpu.VMEM((tm, d), F32)],
        compiler_params=_cparams(("parallel", "arbitrary")),
        name="ffn",
    )(*args)


def _page_copy(cache_ref, layer, page, r0, nrows, dst, sem):
    return pltpu.make_async_copy(cache_ref.at[layer, page, pl.ds(r0, nrows), :], dst, sem)


def _cmp_sample_kernel(pt_ref, cn_ref, pet_ref, w_ref, o_ref, cbuf, sem, *, layer, n_pages):
    b = pl.program_id(0)

    def copy(p):
        return _page_copy(cn_ref, layer, pt_ref[b, p], 0, 256, cbuf.at[:, p, :], sem)

    def start(p, c):
        copy(p).start()
        return c

    def wait(p, c):
        copy(p).wait()
        return c

    lax.fori_loop(0, n_pages, start, 0)
    lax.fori_loop(0, n_pages, wait, 0)

    for c in range(2):
        def lhs_of(d):
            pe = pet_ref[pl.ds(d, 1), :]
            rows = [cbuf[c * 128 + g * 64 + d] for g in range(G_NSA)]
            return (jnp.concatenate(rows, axis=0) + pe).astype(BF16)

        def body(d2, acc):
            lhs = jnp.concatenate([lhs_of(2 * d2), lhs_of(2 * d2 + 1)], axis=1)
            return acc + jnp.dot(lhs, w_ref[c, d2], preferred_element_type=F32)

        acc = lax.fori_loop(0, HEAD_DIM // 2, body, jnp.zeros((G_NSA * n_pages, LANE), F32), unroll=4)
        o_ref[0, c] = acc.astype(BF16)


def _cmp_sample(page_table, cn, pet, wcd, layer):
    db, n_pages = page_table.shape
    return pl.pallas_call(
        functools.partial(_cmp_sample_kernel, layer=layer, n_pages=n_pages),
        grid_spec=pltpu.PrefetchScalarGridSpec(
            num_scalar_prefetch=1, grid=(db,),
            in_specs=[pl.BlockSpec(memory_space=pl.ANY),
                      pl.BlockSpec((HEAD_DIM, LANE), lambda i, pt: (0, 0)),
                      pl.BlockSpec((2, HEAD_DIM // 2, 2 * LANE, LANE), lambda i, pt: (0, 0, 0, 0))],
            out_specs=pl.BlockSpec((1, 2, G_NSA * n_pages, LANE), lambda i, pt: (i, 0, 0, 0)),
            scratch_shapes=[pltpu.VMEM((256, n_pages, LANE), F32), pltpu.SemaphoreType.DMA(())]),
        out_shape=jax.ShapeDtypeStruct((db, 2, G_NSA * n_pages, LANE), BF16),
        compiler_params=_cparams(("arbitrary",)),
        name="cmp_sample",
    )(page_table, cn, pet, wcd)


def _softmax2(parts):
    m = functools.reduce(jnp.maximum, [jnp.max(s, axis=-1, keepdims=True) for s in parts])
    es = [jnp.exp(s - m) for s in parts]
    d = functools.reduce(jnp.add, [jnp.sum(e, axis=-1, keepdims=True) for e in es])
    return es, d


def _pad_rows(x, n):
    return jnp.concatenate([x, jnp.zeros((n - x.shape[0],) + x.shape[1:], x.dtype)], axis=0)


def _nsa_sample_kernel(pt_ref, q_ref, gate_ref, kc_ref, cn_ref, wst_ref, slcn_ref, winn_ref, eb_ref, o_ref,
                       kbuf, vbuf, sem, *, layer, n_pages, past, dseq, rp):
    b = pl.program_id(0)
    nh = H_NSA
    r = nh * rp

    def copies(p):
        page = pt_ref[b, p]
        win = pl.ds(pl.multiple_of(p * LANE, LANE), LANE)
        return (_page_copy(cn_ref, layer, page, 256, 128, kbuf.at[:, win], sem),
                _page_copy(cn_ref, layer, page, 384, 128, vbuf.at[:, win], sem))

    def start(p, c):
        for cp in copies(p):
            cp.start()
        return c

    def wait(p, c):
        for cp in copies(p):
            cp.wait()
        return c

    lax.fori_loop(0, n_pages, start, 0)

    lane = lax.broadcasted_iota(jnp.int32, (rp, LANE), 1)
    row = lax.broadcasted_iota(jnp.int32, (rp, LANE), 0)
    tok = jnp.minimum(row, dseq - 1)
    rr = lax.broadcasted_iota(jnp.int32, (r, 1), 0)
    hrow = rr >> 3
    grow = hrow >> 2
    slope = lax.bitcast_convert_type((126 - hrow) << 23, F32)
    qpos_r = past + jnp.minimum(rr & 7, dseq - 1)

    qs, qsw = [], []
    for g in range(G_NSA):
        for j in range(HPG):
            q32 = q_ref[0, :, j * LANE:(j + 1) * LANE].astype(F32)
            qm = jnp.where((lane >> 6) == g, q32, 0.0)
            qs.append(qm)
            qsw.append(pltpu.roll(qm, HEAD_DIM, 1))
    qm_all = jnp.concatenate(qs, axis=0)
    qsw_all = jnp.concatenate(qsw, axis=0)
    q_lo = jnp.where(grow == 0, qm_all, qsw_all).astype(BF16)
    q_hi = jnp.where(grow == 0, qsw_all, qm_all).astype(BF16)
    qm_bf = qm_all.astype(BF16)

    lane_r = lax.broadcasted_iota(jnp.int32, (r, LANE), 1)
    own = (lane_r >> 6) == grow
    kc = kc_ref[0, 0]
    vc = kc_ref[0, 1]
    s_par, vis_par = [], []
    for par, qq in ((0, q_lo), (1, q_hi)):
        blk = 2 * (lane_r & 63) + par
        dist = qpos_r - (blk * NSA_BLOCK + (NSA_BLOCK - 1))
        s_par.append(_nt(qq, kc) - slope * dist.astype(F32))
        vis_par.append(own & (dist >= 0) & ((lane_r & 63) < n_pages))
    sm = [jnp.where(v, s, NEG) for s, v in zip(s_par, vis_par)]
    mx = jnp.maximum(jnp.max(sm[0], axis=-1, keepdims=True), jnp.max(sm[1], axis=-1, keepdims=True))
    es = [jnp.where(v, jnp.exp(s - mx), 0.0) for s, v in zip(sm, vis_par)]
    den = jnp.maximum(jnp.sum(es[0], axis=-1, keepdims=True) + jnp.sum(es[1], axis=-1, keepdims=True), 1e-30)
    ps = [e / den for e in es]
    a0 = jnp.dot(ps[0].astype(BF16), vc, preferred_element_type=F32)
    a1 = jnp.dot(ps[1].astype(BF16), vc, preferred_element_type=F32)
    cmix = jnp.where(lane_r < HEAD_DIM, a0, a1)
    o_cmp = cmix + pltpu.roll(cmix, HEAD_DIM, 1)

    cur = past // NSA_BLOCK
    selms = []
    for g in range(G_NSA):
        sc, ids, oks = [], [], []
        for par in range(2):
            p4 = ps[par].reshape(nh, rp, LANE)
            imp = p4[g * HPG] + p4[g * HPG + 1] + p4[g * HPG + 2] + p4[g * HPG + 3]
            blk = 2 * (lane & 63) + par
            ok = ((lane >> 6) == g) & ((lane & 63) < n_pages)
            forced = (blk == 0) | (blk == cur - 1)
            sc.append(jnp.where(ok, jnp.where(forced, FORCE_SCORE, imp), -2.0))
            ids.append(blk + jnp.where(ok, 0, 1 << 20) + (lane >> 6) * (1 << 10))
            oks.append(ok)
        sel = _topk_select(jnp.concatenate(sc, axis=1), jnp.concatenate(ids, axis=1), NSA_TOPK - 1)
        selms.append(jnp.where(jnp.concatenate(oks, axis=1), sel - 1.0, 0.0))
    selm = jnp.concatenate(selms, axis=0).astype(BF16)

    lax.fori_loop(0, n_pages, wait, 0)
    nk = n_pages * LANE
    s_p = jnp.dot(qm_bf, kbuf[...].astype(BF16), preferred_element_type=F32)
    kpos = lax.broadcasted_iota(jnp.int32, (r, nk), 1)
    mb = jnp.dot(selm, eb_ref[...], preferred_element_type=F32)
    mb = jnp.broadcast_to(mb.reshape(G_NSA, 1, rp, nk), (G_NSA, HPG, rp, nk)).reshape(r, nk)
    s_p = s_p - slope * (qpos_r - kpos).astype(F32) + mb
    kn = _pad_rows(slcn_ref[0, :, 0:LANE], LANE)
    vn = _pad_rows(slcn_ref[0, :, LANE:2 * LANE], LANE)
    s_n = _nt(qm_bf, kn)
    kposn = past + lane_r
    okn = (lane_r < dseq) & (kposn <= qpos_r)
    s_n = jnp.where(okn, s_n - slope * (qpos_r - kposn).astype(F32), -BIG)
    (e_p, e_n), den = _softmax2([s_p, s_n])
    o_slc = (_nt(e_p.astype(BF16), vbuf[...].astype(BF16))
             + jnp.dot(e_n.astype(BF16), vn, preferred_element_type=F32)) / den

    wbs = wst_ref.shape[3]
    s_w = jnp.dot(qm_bf, wst_ref[0, 0, 0:LANE, :].astype(BF16), preferred_element_type=F32)
    kposw = (past - wbs) + lax.broadcasted_iota(jnp.int32, (r, wbs), 1)
    dw = qpos_r - kposw
    s_w = jnp.where((dw >= 0) & (dw < NSA_WINDOW) & (kposw >= 0), s_w - slope * dw.astype(F32), -BIG)
    kwn = _pad_rows(winn_ref[0, :, 0:LANE], LANE)
    vwn = _pad_rows(winn_ref[0, :, LANE:2 * LANE], LANE)
    dn = qpos_r - kposn
    s_wn = jnp.where((lane_r < dseq) & (dn >= 0) & (dn < NSA_WINDOW), _nt(qm_bf, kwn) - slope * dn.astype(F32), -BIG)
    (e_w, e_wn), denw = _softmax2([s_w, s_wn])
    o_win = (_nt(e_w.astype(BF16), wst_ref[0, 0, LANE:2 * LANE, :].astype(BF16))
             + jnp.dot(e_wn.astype(BF16), vwn, preferred_element_type=F32)) / denw

    gate = gate_ref[0]
    o_cmp = o_cmp.reshape(nh, rp, LANE)
    o_slc = o_slc.reshape(nh, rp, LANE)
    o_win = o_win.reshape(nh, rp, LANE)
    for j in range(HPG):
        parts = []
        for g in range(G_NSA):
            hh = g * HPG + j
            c = 2 * j + g
            parts.append(gate[:, c:c + 1] * o_cmp[hh] + gate[:, 8 + c:9 + c] * o_slc[hh]
                         + gate[:, 16 + c:17 + c] * o_win[hh])
        o_ref[0, :, j * LANE:(j + 1) * LANE] = jnp.where(lane < HEAD_DIM, parts[0], parts[1])


def _nsa_sample(page_table, qn, small, kcs, cn, wst, slcn, winn, eb, layer, past, dseq):
    db, n_pages = page_table.shape
    rp = qn.shape[1]
    nk = n_pages * LANE
    wbs = wst.shape[3]
    kern = functools.partial(_nsa_sample_kernel, layer=layer, n_pages=n_pages, past=past, dseq=dseq, rp=rp)
    return pl.pallas_call(
        kern,
        grid_spec=pltpu.PrefetchScalarGridSpec(
            num_scalar_prefetch=1, grid=(db,),
            in_specs=[pl.BlockSpec((1, rp, W_Q), lambda i, pt: (i, 0, 0)),
                      pl.BlockSpec((1, rp, LANE), lambda i, pt: (i, 0, 0)),
                      pl.BlockSpec((1, 2, LANE, LANE), lambda i, pt: (i, 0, 0, 0)),
                      pl.BlockSpec(memory_space=pl.ANY),
                      pl.BlockSpec((1, 1, 2 * LANE, wbs), lambda i, pt: (layer, i, 0, 0)),
                      pl.BlockSpec((1, rp, 2 * LANE), lambda i, pt: (i, 0, 0)),
                      pl.BlockSpec((1, rp, 2 * LANE), lambda i, pt: (i, 0, 0)),
                      pl.BlockSpec((2 * LANE, nk), lambda i, pt: (0, 0))],
            out_specs=pl.BlockSpec((1, rp, W_Q), lambda i, pt: (i, 0, 0)),
            scratch_shapes=[pltpu.VMEM((LANE, nk), F32), pltpu.VMEM((LANE, nk), F32),
                            pltpu.SemaphoreType.DMA(())]),
        out_shape=jax.ShapeDtypeStruct((db, rp, W_Q), F32),
        compiler_params=_cparams(("arbitrary",)),
        name="nsa_sample",
    )(page_table, qn, small, kcs, cn, wst, slcn, winn, eb)


def _logf_sample_kernel(pt_ref, cl_ref, utri_ref, sl_ref, o_ref, lbuf, sem, *, layer, n_pages):
    b = pl.program_id(0)

    def copy(p):
        return pltpu.make_async_copy(cl_ref.at[layer, pt_ref[b, p]], lbuf.at[pl.ds(p * H_FOX, H_FOX), :], sem)

    def start(p, c):
        copy(p).start()
        return c

    def wait(p, c):
        copy(p).wait()
        return c

    lax.fori_loop(0, n_pages, start, 0)
    lax.fori_loop(0, n_pages, wait, 0)

    def split3(x):
        hi = x.astype(BF16)
        rem = x - hi.astype(F32)
        mid = rem.astype(BF16)
        return hi, mid, (rem - mid.astype(F32)).astype(BF16)

    utri = utri_ref[...]
    w = jnp.zeros(lbuf.shape, F32)
    for part in split3(lbuf[...]):
        w = w + jnp.dot(part, utri, preferred_element_type=F32)
    tot = jnp.broadcast_to(w[:, LANE - 1:LANE], w.shape)
    carry = jnp.zeros(lbuf.shape, F32)
    sl = sl_ref[...]
    for part in split3(tot):
        carry = carry + jnp.dot(sl, part, preferred_element_type=F32)
    o_ref[0] = w + carry


def _logf_sample(page_table, cl, utri, slm, layer):
    db, n_pages = page_table.shape
    rows = n_pages * H_FOX
    return pl.pallas_call(
        functools.partial(_logf_sample_kernel, layer=layer, n_pages=n_pages),
        grid_spec=pltpu.PrefetchScalarGridSpec(
            num_scalar_prefetch=1, grid=(db,),
            in_specs=[pl.BlockSpec(memory_space=pl.ANY),
                      pl.BlockSpec((LANE, LANE), lambda i, pt: (0, 0)),
                      pl.BlockSpec((rows, rows), lambda i, pt: (0, 0))],
            out_specs=pl.BlockSpec((1, rows, LANE), lambda i, pt: (i, 0, 0)),
            scratch_shapes=[pltpu.VMEM((rows, LANE), F32), pltpu.SemaphoreType.DMA(())]),
        out_shape=jax.ShapeDtypeStruct((db, rows, LANE), F32),
        compiler_params=_cparams(("arbitrary",)),
        name="logf_sample",
    )(page_table, cl, utri, slm)


def _fox_sample_kernel(pt_ref, q_ref, c_ref, clast_ref, lfn_ref, kn_ref, vn_ref, place_ref, cf_ref, o_ref,
                       kbuf, vbuf, sem, *, layer, n_pages, dseq, rp):
    b = pl.program_id(0)
    pair = pl.program_id(1)
    npair = pl.num_programs(1)
    half = H_FOX * HEAD_DIM
    step = b * npair + pair
    slot = step % 2

    def copies(bb, pp, sl, p):
        page = pt_ref[bb, p]
        win = pl.ds(pl.multiple_of(p * LANE, LANE), LANE)
        r0 = pl.multiple_of(pp * LANE, LANE)
        return (_page_copy(cf_ref, layer, page, r0, LANE, kbuf.at[sl, :, win], sem.at[sl]),
                _page_copy(cf_ref, layer, page, half + r0, LANE, vbuf.at[sl, :, win], sem.at[sl]))

    def start_all(bb, pp, sl):
        def start(p, c):
            for cp in copies(bb, pp, sl, p):
                cp.start()
            return c
        lax.fori_loop(0, n_pages, start, 0)

    @pl.when(step == 0)
    def _():
        start_all(b, pair, slot)

    @pl.when(step + 1 < pl.num_programs(0) * npair)
    def _():
        nxt = step + 1
        start_all(nxt // npair, nxt % npair, 1 - slot)

    def wait(p, c):
        for cp in copies(b, pair, slot, p):
            cp.wait()
        return c

    qa = _qf_aug(q_ref[0], pair, rp).astype(BF16)
    r = 2 * rp
    nk = n_pages * LANE

    rows8 = lax.broadcasted_iota(jnp.int32, (rp, LANE), 0)
    lfn = lfn_ref[0]
    cn = jnp.broadcast_to(clast_ref[0], (rp, LANE))
    for k in range(dseq):
        cn = cn + jnp.where(rows8 >= k, lfn[k:k + 1, :], 0.0)
    hi = cn.astype(BF16)
    rem = cn - hi.astype(F32)
    mid = rem.astype(BF16)
    lo = (rem - mid.astype(F32)).astype(BF16)
    caug = (jnp.dot(hi, place_ref[0], preferred_element_type=F32) + jnp.dot(mid, place_ref[1], preferred_element_type=F32)
            + jnp.dot(lo, place_ref[2], preferred_element_type=F32)).astype(BF16)
    ka_n = _pad_rows(jnp.concatenate([kn_ref[0], caug], axis=1), LANE)
    s_n = _nt(qa, ka_n)
    lane_r = lax.broadcasted_iota(jnp.int32, (r, LANE), 1)
    tok_r = jnp.minimum(lax.broadcasted_iota(jnp.int32, (r, LANE), 0) & (rp - 1), dseq - 1)
    s_n = jnp.where((lane_r < dseq) & (lane_r <= tok_r), s_n, -BIG)

    lax.fori_loop(0, n_pages, wait, 0)
    s_p = jnp.dot(qa[:, 0:LANE], kbuf[slot].astype(BF16), preferred_element_type=F32)
    s_p = (s_p.reshape(2, rp, nk) - c_ref[0, 0][:, None, :]).reshape(r, nk)
    (e_p, e_n), den = _softmax2([s_p, s_n])
    o = (_nt(e_p.astype(BF16), vbuf[slot].astype(BF16))
         + jnp.dot(e_n.astype(BF16), _pad_rows(vn_ref[0], LANE), preferred_element_type=F32)) / den
    o = o.reshape(2, rp, LANE)
    lane = lax.broadcasted_iota(jnp.int32, (rp, LANE), 1)
    o_ref[0] = jnp.where(lane < HEAD_DIM, o[0], o[1])


def _fox_sample(page_table, qf, cflat, clast, small, kvfb, place, cf, layer, dseq):
    db, n_pages = page_table.shape
    rp = qf.shape[1]
    npair = H_FOX // 2
    nk = n_pages * LANE
    kern = functools.partial(_fox_sample_kernel, layer=layer, n_pages=n_pages, dseq=dseq, rp=rp)
    return pl.pallas_call(
        kern,
        grid_spec=pltpu.PrefetchScalarGridSpec(
            num_scalar_prefetch=1, grid=(db, npair),
            in_specs=[pl.BlockSpec((1, rp, LANE), lambda i, p, pt: (i, 0, p)),
                      pl.BlockSpec((1, 1, 2, nk), lambda i, p, pt: (i, p, 0, 0)),
                      pl.BlockSpec((1, 1, LANE), lambda i, p, pt: (i, 0, 0)),
                      pl.BlockSpec((1, rp, LANE), lambda i, p, pt: (i, 0, 0)),
                      pl.BlockSpec((1, rp, LANE), lambda i, p, pt: (i, 0, p)),
                      pl.BlockSpec((1, rp, LANE), lambda i, p, pt: (i, 0, npair + p)),
                      pl.BlockSpec((3, LANE, LANE), lambda i, p, pt: (0, 0, 0)),
                      pl.BlockSpec(memory_space=pl.ANY)],
            out_specs=pl.BlockSpec((1, rp, LANE), lambda i, p, pt: (i, 0, p)),
            scratch_shapes=[pltpu.VMEM((2, LANE, nk), F32), pltpu.VMEM((2, LANE, nk), F32),
                            pltpu.SemaphoreType.DMA((2,))]),
        out_shape=jax.ShapeDtypeStruct((db, rp, W_QF), F32),
        compiler_params=_cparams(("arbitrary", "arbitrary")),
        name="fox_sample",
    )(page_table, qf, cflat, clast, small, kvfb, kvfb, place, cf)


def _perm_heads(x, axis):
    shp = x.shape
    x = x.reshape(shp[:axis] + (G_NSA, HPG) + shp[axis + 1:])
    x = jnp.swapaxes(x, axis, axis + 1)
    return x.reshape(shp)


def _prep_layer_weights(w_in, b_gate, b_forget, cmp_pe, w_cmp_k, w_cmp_v, g_group, w_out):
    d = w_in.shape[0]
    o = 0
    qn = w_in[:, o:o + W_Q]; o += W_Q
    kvn = w_in[:, o:o + W_KVN]; o += W_KVN
    gate = w_in[:, o:o + W_GATE]; o += W_GATE
    qf = w_in[:, o:o + W_QF]; o += W_QF
    kvf = w_in[:, o:o + W_KVF]; o += W_KVF
    fg = w_in[:, o:o + H_FOX]
    qn = _perm_heads(qn.reshape(d, H_NSA, HEAD_DIM), 1).reshape(d, W_Q)
    gate = _perm_heads(gate.reshape(d, 3, H_NSA), 2).reshape(d, W_GATE)
    small = jnp.concatenate([gate, fg, jnp.zeros((d, LANE - W_GATE - H_FOX), w_in.dtype)], axis=1)
    w_proj = jnp.concatenate([qn, kvn, qf, kvf, small], axis=1).astype(BF16)
    brow = jnp.concatenate([_perm_heads(b_gate.reshape(3, H_NSA), 1).reshape(W_GATE), b_forget,
                            jnp.zeros((LANE - W_GATE - H_FOX,), F32)]).reshape(1, LANE)
    w4 = jnp.stack([w_cmp_k, w_cmp_k, w_cmp_v, w_cmp_v])
    w_big = jnp.einsum("xc,clde->lxdce", jnp.eye(4, dtype=F32), w4)
    w_big = w_big.reshape(NSA_BLOCK * 4 * HEAD_DIM, 4 * HEAD_DIM).astype(BF16)
    pe_row = jnp.broadcast_to(cmp_pe[:, None, :], (NSA_BLOCK, 4, HEAD_DIM))
    pe_rows = jnp.zeros((8, NSA_BLOCK * 4 * HEAD_DIM), F32).at[0].set(pe_row.reshape(-1)).astype(BF16)
    gg = jnp.concatenate([_perm_heads(g_group[:W_Q].reshape(H_NSA, HEAD_DIM), 0).reshape(W_Q),
                          g_group[W_Q:]]).reshape(1, -1)
    w_o = jnp.concatenate([_perm_heads(w_out[:W_Q].reshape(H_NSA, HEAD_DIM, -1), 0).reshape(W_Q, -1),
                           w_out[W_Q:]], axis=0).astype(BF16)
    return w_proj, brow, w_big, pe_rows, gg, w_o


def _pos_aug(pos):
    n = pos.shape[0]
    z = jnp.zeros((n, LANE), F32)
    z = z.at[:, 0].set((pos // NSA_BLOCK).astype(F32)).at[:, 1].set((pos % NSA_BLOCK).astype(F32))
    return z.at[:, 2].set(1.0).at[:, 3].set(1.0).astype(BF16)


def _expand_t(nkeys, nblk_pad):
    kb = jnp.arange(nkeys)[:, None] // NSA_BLOCK
    return jnp.where(kb == jnp.arange(nblk_pad)[None, :], BIG, 0.0).astype(BF16)


def _fox_consts(tc):
    ltri = (jnp.arange(tc)[:, None] >= jnp.arange(tc)[None, :]).astype(BF16)
    place = np.zeros((3, LANE, LANE), np.float32)
    for i in range(3):
        for h in range(H_FOX):
            place[i, LOGF_LANE + h, 8 * (h // 2) + 2 * i + (h % 2)] = 1.0
    return ltri, jnp.asarray(place, BF16)


def _mixer_prompt(x, lw, consts, b, seq, tiles):
    w_proj, brow, w_big, pe_rows, gg, w_o, g_pre, g_post = lw
    m = b * seq
    qn, nsakv, winf, cmpb, slcb, winb, qf, kvf, kvfb, small = _proj(x, g_pre, w_proj, brow, tiles["tm"], seq)
    nblk = seq // NSA_BLOCK
    nb = b * nblk
    kvc = _compress(cmpb.reshape(nb, NSA_BLOCK * 256), pe_rows, w_big, min(256, nb), 4096).reshape(b, nblk, 256)
    if nblk < LANE:
        kvc = jnp.pad(kvc, ((0, 0), (0, LANE - nblk), (0, 0)))
    o_n = _nsa_prompt(qn.reshape(b, seq, W_Q), small.reshape(b, seq, LANE), kvc, consts["cpos"],
                      slcb.reshape(b, seq, 256), winb.reshape(b, seq, 256), consts["kpos"], consts["ebt"],
                      tiles["nsa_tq"], tiles["nsa_tk"], tiles["nsa_chunk"])
    caug = _fox_prep(small.reshape(b, seq, LANE), consts["ltri"], consts["place"], tiles["tc"])
    o_f = _fox_prompt(qf.reshape(b, seq, W_QF), kvfb.reshape(b, seq, W_KVF), caug, tiles["fox_tq"],
                      tiles["fox_tk"], tiles["fox_chunk"])
    y = _mixout(o_n.reshape(m, W_Q), o_f.reshape(m, W_QF), x, gg, w_o, g_post, tiles["tm"])
    return y, nsakv, winf, kvf, small


RP = 8


def _sample_consts(n_pages, w_cmp_k, w_cmp_v, cmp_pe):
    w2 = jnp.stack([w_cmp_k, w_cmp_v])
    wcd = jnp.einsum("xy,clde->cdxlye", jnp.eye(2, dtype=F32), w2)
    wcd = wcd.reshape(2, HEAD_DIM // 2, 4 * NSA_BLOCK, 2 * HEAD_DIM).astype(BF16)
    pet = jnp.tile(cmp_pe.T, (1, 2))
    return wcd, pet


def _expand_sample(n_pages):
    rowi = jnp.arange(2 * LANE)
    blk = 2 * (rowi % 64) + rowi // LANE
    ok = (rowi % 64) < n_pages
    kb = jnp.arange(n_pages * LANE) // NSA_BLOCK
    return jnp.where((blk[:, None] == kb[None, :]) & ok[:, None], BIG, 0.0).astype(BF16)


def _logf_consts(n_pages):
    utri = (jnp.arange(LANE)[:, None] <= jnp.arange(LANE)[None, :]).astype(BF16)
    ri = jnp.arange(n_pages * H_FOX)
    slm = ((ri[:, None] % H_FOX == ri[None, :] % H_FOX) & (ri[None, :] // H_FOX < ri[:, None] // H_FOX)).astype(BF16)
    return utri, slm


def _mixer_sample(x, lw, sc, views, page_table, layer, past, dseq):
    w_proj, brow, _, _, gg, w_o, g_pre, g_post = lw
    cn, cf, cl, wst = views
    db, n_pages = page_table.shape
    m = db * RP
    qn, nsakv, winf, _, slcb, winb, qf, kvf, kvfb, small = _proj(x, g_pre, w_proj, brow, m)
    kcs = _cmp_sample(page_table, cn, sc["pet"], sc["wcd"], layer)
    if n_pages < 64:
        kcs = jnp.pad(kcs.reshape(db, 2, G_NSA, n_pages, LANE), ((0, 0),) * 3 + ((0, 64 - n_pages), (0, 0)))
        kcs = kcs.reshape(db, 2, LANE, LANE)
    o_n = _nsa_sample(page_table, qn.reshape(db, RP, W_Q), small.reshape(db, RP, LANE), kcs, cn, wst,
                      slcb.reshape(db, RP, 256), winb.reshape(db, RP, 256), sc["eb"], layer, past, dseq)
    c_rows = _logf_sample(page_table, cl, sc["utri"], sc["slm"], layer)
    cflat = c_rows.reshape(db, n_pages, H_FOX // 2, 2, LANE).transpose(0, 2, 3, 1, 4)
    cflat = cflat.reshape(db, H_FOX // 2, 2, n_pages * LANE)
    clast = jnp.pad(cflat[:, :, :, -1].reshape(db, 1, H_FOX), ((0, 0), (0, 0), (LOGF_LANE, LANE - LOGF_LANE - H_FOX)))
    o_f = _fox_sample(page_table, qf.reshape(db, RP, W_QF), cflat, clast, small.reshape(db, RP, LANE),
                      kvfb.reshape(db, RP, W_KVF), sc["place"], cf, layer, dseq)
    y = _mixout(o_n.reshape(m, W_Q), o_f.reshape(m, W_QF), x, gg, w_o, g_post, m)
    return y, nsakv, winf, kvf, small


def _tiles(seq):
    return dict(tm=min(512, seq), nsa_tq=256, nsa_tk=min(512, seq), tc=min(256, seq), fox_tq=min(2048, seq),
                fox_tk=min(512, seq), fox_chunk=256, nsa_chunk=256, ffn_tm=min(512, seq))


def kernel(x_prompt, x_sample, cache_nsa_kv, state_nsa_win, cache_fox_kv, cache_fox_logf, state_conv, page_table,
           g_attn_pre, g_attn_post, g_ffn_pre, g_ffn_post, w_mix_in, b_gate, b_forget, cmp_pe, w_cmp_k, w_cmp_v,
           g_group, w_mix_out, w_ffn_in, conv_w, conv_b, w_ffn_out):
    depth = w_mix_in.shape[0]
    b, seq, d = x_prompt.shape
    db, dseq, _ = x_sample.shape
    dff = conv_b.shape[1]
    wb = min(NSA_WINDOW, seq)
    tiles = _tiles(seq)
    tf = dff // 2
    ltri, place = _fox_consts(tiles["tc"])
    consts = dict(cpos=_pos_aug(jnp.arange(LANE) * NSA_BLOCK + (NSA_BLOCK - 1)), kpos=_pos_aug(jnp.arange(seq)),
                  ebt=_expand_t(seq, LANE), ltri=ltri, place=place)

    n_pages, page = page_table.shape[1], cache_nsa_kv.shape[2]
    past = n_pages * page
    wbs = state_nsa_win.shape[2]
    assert page == LANE and past % NSA_BLOCK == 0 and dseq <= min(RP, NSA_BLOCK) and n_pages <= 64
    cn = jnp.transpose(cache_nsa_kv, (0, 1, 3, 4, 5, 2)).reshape(depth, -1, 4 * G_NSA * HEAD_DIM, page)
    cf = jnp.transpose(cache_fox_kv, (0, 1, 3, 4, 5, 2)).reshape(depth, -1, W_KVF, page)
    cl = jnp.transpose(cache_fox_logf, (0, 1, 3, 2))
    wst = jnp.transpose(state_nsa_win, (0, 1, 3, 4, 5, 2)).reshape(depth, db, 2 * G_NSA * HEAD_DIM, wbs)
    views = (cn, cf, cl, wst)
    utri, slm = _logf_consts(n_pages)
    eb_s = _expand_sample(n_pages)
    xs = jnp.pad(x_sample, ((0, 0), (0, RP - dseq), (0, 0))).reshape(db * RP, d)
    s_nsa, s_win, s_fox, s_logf, s_conv = [], [], [], [], []

    xp = x_prompt.reshape(b * seq, d)
    p_nsa, p_win, p_fox, p_logf, p_conv = [], [], [], [], []
    for l in range(depth):
        w_proj, brow, w_big, pe_rows, gg, w_o = _prep_layer_weights(
            w_mix_in[l], b_gate[l], b_forget[l], cmp_pe[l], w_cmp_k[l], w_cmp_v[l], g_group[l], w_mix_out[l])
        lw = (w_proj, brow, w_big, pe_rows, gg, w_o, g_attn_pre[l].reshape(1, d), g_attn_post[l].reshape(1, d))
        wa = w_ffn_in[l, :, :dff].astype(BF16)
        wbb = w_ffn_in[l, :, dff:].astype(BF16)
        wo = w_ffn_out[l].astype(BF16)
        ffn_w = (g_ffn_pre[l].reshape(1, d), wa, wbb, conv_w[l], conv_b[l].reshape(1, dff), wo,
                 g_ffn_post[l].reshape(1, d))

        xp, nsakv, winf, kvf, small = _mixer_prompt(xp, lw, consts, b, seq, tiles)
        xp, a_tail = _ffn(xp, None, *ffn_w, tm=tiles["ffn_tm"], tf=tf, seq=seq, tail=8)
        p_nsa.append(nsakv.reshape(b, 4, G_NSA, HEAD_DIM, seq).transpose(0, 4, 1, 2, 3))
        p_win.append(winf[:, :, seq - wb:].reshape(b, 2, G_NSA, HEAD_DIM, wb).transpose(0, 4, 1, 2, 3))
        p_fox.append(kvf.reshape(b, 2, H_FOX, HEAD_DIM, seq).transpose(0, 4, 1, 2, 3))
        p_logf.append(small.reshape(b, seq, LANE)[:, :, LOGF_LANE:LOGF_LANE + H_FOX])
        nbk = seq // tiles["ffn_tm"]
        p_conv.append(a_tail.reshape(b, nbk, 8, dff)[:, nbk - 1, 8 - (CONV_W - 1):])

        wcd, pet = _sample_consts(n_pages, w_cmp_k[l], w_cmp_v[l], cmp_pe[l])
        sc = dict(wcd=wcd, pet=pet, eb=eb_s, utri=utri, slm=slm, place=place)
        xs, nsakv, winf, kvf, small = _mixer_sample(xs, lw, sc, views, page_table, l, past, dseq)
        st = jnp.zeros((db + 1, RP, dff), F32).at[:db, RP - (CONV_W - 1):].set(state_conv[l])
        xs, a_full = _ffn(xs, st.reshape((db + 1) * RP, dff), *ffn_w, tm=db * RP, tf=tf, seq=RP, tail=db * RP)
        s_nsa.append(nsakv.reshape(db, RP, 4, G_NSA, HEAD_DIM)[:, :dseq])
        new_win = winf.reshape(db, RP, 2, G_NSA, HEAD_DIM)[:, :dseq]
        s_win.append(jnp.concatenate([state_nsa_win[l], new_win], axis=1)[:, dseq:])
        s_fox.append(kvf.reshape(db, RP, 2, H_FOX, HEAD_DIM)[:, :dseq])
        s_logf.append(small.reshape(db, RP, LANE)[:, :dseq, LOGF_LANE:LOGF_LANE + H_FOX])
        a_rows = jnp.concatenate([state_conv[l], a_full.reshape(db, RP, dff)[:, :dseq]], axis=1)
        s_conv.append(a_rows[:, dseq:])

    return (xp.reshape(b, seq, d), xs.reshape(db, RP, d)[:, :dseq],
            jnp.stack(p_nsa), jnp.stack(s_nsa),
            jnp.stack(p_win), jnp.stack(s_win),
            jnp.stack(p_fox), jnp.stack(s_fox),
            jnp.stack(p_logf), jnp.stack(s_logf),
            jnp.stack(p_conv), jnp.stack(s_conv))
```

```python
import functools

import jax
import jax.numpy as jnp
import numpy as np
from jax import lax
from jax.experimental import pallas as pl
from jax.experimental.pallas import tpu as pltpu

F32 = jnp.float32
BF16 = jnp.bfloat16

HEAD_DIM = 64
G_NSA = 2
HPG = 4
H_NSA = G_NSA * HPG
H_FOX = 8
NSA_BLOCK = 64
NSA_TOPK = 16
NSA_WINDOW = 512
CONV_W = 3
RMS_EPS = 1e-6
FORCE_SCORE = 1e4
SCALE = HEAD_DIM ** -0.5

LANE = 128
VMEM_LIMIT = 56 * 1024 * 1024
NEG = -1e30
BIG = 2.0 ** 100
M_INIT = -1e38

W_Q = H_NSA * HEAD_DIM
W_KVN = 6 * G_NSA * HEAD_DIM
W_GATE = 3 * H_NSA
W_QF = H_FOX * HEAD_DIM
W_KVF = 2 * H_FOX * HEAD_DIM
C_QN, C_KVN, C_QF, C_KVF, C_SMALL = 0, 512, 1280, 1792, 2816
N_PROJ = C_SMALL + LANE
LOGF_LANE = W_GATE


def _cparams(sem):
    return pltpu.CompilerParams(dimension_semantics=sem, vmem_limit_bytes=VMEM_LIMIT)


def _nt(a, b):
    return lax.dot_general(a, b, (((1,), (1,)), ((), ())), preferred_element_type=F32)


def _rms(x, g):
    return x * lax.rsqrt(jnp.mean(x * x, axis=-1, keepdims=True) + RMS_EPS) * g


def _proj_kernel(x_ref, g_ref, w_ref, b_ref, qn_ref, nsakv_ref, winf_ref, cmpb_ref, slcb_ref, winb_ref,
                 qf_ref, kvf_ref, kvfb_ref, small_ref, *, feature_major):
    h = _rms(x_ref[...], g_ref[...]).astype(BF16)

    def mm(c0, c1):
        return jnp.dot(h, w_ref[:, c0:c1], preferred_element_type=F32)

    def put(ref, u):
        if feature_major:
            ref[0] = u.T
        else:
            ref[...] = u

    qn_ref[...] = (mm(C_QN, C_QN + W_Q) * SCALE).astype(BF16)
    u = mm(C_KVN, C_KVN + 512)
    put(nsakv_ref, u)
    cmpb_ref[...] = u[:, 0:256].astype(BF16)
    slcb_ref[...] = u[:, 256:512].astype(BF16)
    u = mm(C_KVN + 512, C_KVN + 768)
    put(winf_ref, u)
    winb_ref[...] = u.astype(BF16)
    qf_ref[...] = (mm(C_QF, C_QF + W_QF) * SCALE).astype(BF16)
    u = mm(C_KVF, C_KVF + W_KVF)
    put(kvf_ref, u)
    kvfb_ref[...] = u.astype(BF16)
    z = mm(C_SMALL, N_PROJ) + b_ref[...]
    lane = lax.broadcasted_iota(jnp.int32, z.shape, 1)
    log_sig = jnp.minimum(z, 0.0) - jnp.log(1.0 + jnp.exp(-jnp.abs(z)))
    small_ref[...] = jnp.where(lane < W_GATE, jax.nn.sigmoid(z), log_sig)


def _proj(x, g, w, brow, tm, seq=None):
    m, d = x.shape
    fm = seq is not None
    widths = [(W_Q, BF16, False), (512, F32, fm), (256, F32, fm), (256, BF16, False), (256, BF16, False),
              (256, BF16, False), (W_QF, BF16, False), (W_KVF, F32, fm), (W_KVF, BF16, False), (LANE, F32, False)]
    nblk = (seq // tm) if fm else 1

    def spec(wd, t):
        if t:
            return pl.BlockSpec((1, wd, tm), lambda i: (i // nblk, 0, i % nblk))
        return pl.BlockSpec((tm, wd), lambda i: (i, 0))

    def shape(wd, dt, t):
        return jax.ShapeDtypeStruct((m // seq, wd, seq) if t else (m, wd), dt)

    return pl.pallas_call(
        functools.partial(_proj_kernel, feature_major=fm),
        grid=(m // tm,),
        in_specs=[pl.BlockSpec((tm, d), lambda i: (i, 0)),
                  pl.BlockSpec((1, d), lambda i: (0, 0)),
                  pl.BlockSpec((d, N_PROJ), lambda i: (0, 0)),
                  pl.BlockSpec((1, LANE), lambda i: (0, 0))],
        out_specs=[spec(wd, t) for wd, _, t in widths],
        out_shape=[shape(wd, dt, t) for wd, dt, t in widths],
        compiler_params=_cparams(("parallel",)),
        name="proj",
    )(x, g, w, brow)


def _compress_kernel(x_ref, pe_ref, w_ref, o_ref, acc_ref, accpe_ref):
    k = pl.program_id(1)

    @pl.when(k == 0)
    def _():
        acc_ref[...] = jnp.zeros_like(acc_ref)
        accpe_ref[...] = jnp.zeros_like(accpe_ref)

    w = w_ref[...]
    acc_ref[...] += jnp.dot(x_ref[...].astype(BF16), w, preferred_element_type=F32)
    accpe_ref[...] += jnp.dot(pe_ref[...], w, preferred_element_type=F32)

    @pl.when(k == pl.num_programs(1) - 1)
    def _():
        o_ref[...] = (acc_ref[...] + accpe_ref[0:1, :]).astype(BF16)


def _compress(x, pe_rows, w_big, tmb, tkc):
    nb, kk = x.shape
    return pl.pallas_call(
        _compress_kernel,
        grid=(nb // tmb, kk // tkc),
        in_specs=[pl.BlockSpec((tmb, tkc), lambda i, k: (i, k)),
                  pl.BlockSpec((8, tkc), lambda i, k: (0, k)),
                  pl.BlockSpec((tkc, 256), lambda i, k: (k, 0))],
        out_specs=pl.BlockSpec((tmb, 256), lambda i, k: (i, 0)),
        out_shape=jax.ShapeDtypeStruct((nb, 256), BF16),
        scratch_shapes=[pltpu.VMEM((tmb, 256), F32), pltpu.VMEM((8, 256), F32)],
        compiler_params=_cparams(("parallel", "arbitrary")),
        name="compress",
    )(x, pe_rows, w_big)


_SKIP = "skip"


def _flash_tile(qa_ref, ka, va, m_ref, acc_ref, chunk, bias_fn=None):
    r = qa_ref.shape[0]
    tk = ka.shape[0]
    for c0 in range(0, r, chunk):
        rows = slice(c0, c0 + chunk)
        bias = None if bias_fn is None else bias_fn(c0)
        if bias is _SKIP:
            continue
        s = _nt(qa_ref[rows, :], ka)
        if bias is not None:
            s = s + bias
        m_prev = m_ref[rows, :]
        m_new = jnp.maximum(m_prev, jnp.max(s, axis=-1, keepdims=True))
        alpha = jnp.exp(m_prev - m_new)
        p = jnp.exp(s - jnp.tile(m_new, (1, tk // LANE)))
        acc_ref[rows, :] = (jnp.tile(alpha, (1, 2)) * acc_ref[rows, :]
                            + jnp.dot(p.astype(BF16), va, preferred_element_type=F32))
        m_ref[rows, :] = m_new


def _with_ones(v):
    return jnp.concatenate([v, jnp.ones_like(v)], axis=1)


def _masked_softmax(s, mask):
    s = jnp.where(mask, s, NEG)
    m = jnp.max(s, axis=-1, keepdims=True)
    e = jnp.where(mask, jnp.exp(s - m), 0.0)
    d = jnp.sum(e, axis=-1, keepdims=True)
    return e / jnp.maximum(d, 1e-30)


def _topk_select(score, blkid, k):
    sel = jnp.zeros_like(score)
    big = jnp.int32(1 << 30)
    for _ in range(k):
        mx = jnp.max(score, axis=-1, keepdims=True)
        idx = jnp.min(jnp.where(score == mx, blkid, big), axis=-1, keepdims=True)
        hit = blkid == idx
        sel = jnp.where(hit, jnp.where(mx >= 0.0, 1.0, 0.0), sel)
        score = jnp.where(hit, -3e38, score)
    return sel


def _topk_select_lanes(score, valid, k):
    st = score.T
    ok = valid.astype(F32).T > 0.0
    blk = lax.broadcasted_iota(jnp.int32, st.shape, 0)
    sel = jnp.zeros_like(st)
    for _ in range(k):
        mx = jnp.max(st, axis=0, keepdims=True)
        first = jnp.min(jnp.where(st == mx, blk, st.shape[0]), axis=0, keepdims=True)
        hit = blk == first
        sel = jnp.where(hit & ok, 1.0, sel)
        st = jnp.where(hit, -3e38, st)
    return sel.T


def _slope(g, j):
    return 2.0 ** (-(g * HPG + j + 1))


def _build_q_aug(q_ref, qa_ref, qpos, tq):
    lane = lax.broadcasted_iota(jnp.int32, (tq, LANE), 1)
    qblk = (qpos >> 6).astype(F32)
    qin = (qpos & 63).astype(F32)
    for g in range(G_NSA):
        for j in range(HPG):
            sl = _slope(g, j)
            qb = q_ref[0, :, j * LANE:(j + 1) * LANE]
            qm = jnp.where((lane >> 6) == g, qb, jnp.zeros_like(qb))
            aug = jnp.where(lane == 0, sl * NSA_BLOCK,
                            jnp.where(lane == 1, sl,
                                      jnp.where(lane == 2, -sl * NSA_BLOCK * qblk,
                                                jnp.where(lane == 3, -sl * qin, 0.0))))
            r0 = (g * HPG + j) * tq
            qa_ref[r0:r0 + tq, :] = jnp.concatenate([qm, aug.astype(BF16)], axis=1)


def _nsa_prompt_kernel(q_ref, gate_ref, kvc_ref, cpos_ref, slc_ref, win_ref, kpos_ref, ebt_ref, o_ref,
                       qa_ref, selm_ref, m_ref, acc_ref, og_ref, *, tq, tk, seq, nblk, chunk):
    qi = pl.program_id(1)
    q0 = qi * tq
    nh = H_NSA
    row = lax.broadcasted_iota(jnp.int32, (tq, LANE), 0)
    lane = lax.broadcasted_iota(jnp.int32, (tq, LANE), 1)
    qpos = q0 + row
    _build_q_aug(q_ref, qa_ref, qpos, tq)

    def add_gated(o, branch, first):
        for g in range(G_NSA):
            for j in range(HPG):
                hh = g * HPG + j
                c = branch * H_NSA + 2 * j + g
                rows = slice(hh * tq, (hh + 1) * tq)
                val = gate_ref[0, :, c:c + 1] * o[rows, :]
                og_ref[rows, :] = val if first else og_ref[rows, :] + val

    kca = jnp.concatenate([kvc_ref[0, :, 0:LANE], cpos_ref[...]], axis=1)
    s = _nt(qa_ref[...], kca).reshape(nh, tq, LANE)
    vis = (lane * NSA_BLOCK + (NSA_BLOCK - 1) <= qpos) & (lane < nblk)
    p = _masked_softmax(s, vis[None])
    add_gated(jnp.dot(p.reshape(nh * tq, LANE).astype(BF16), kvc_ref[0, :, LANE:2 * LANE],
                      preferred_element_type=F32), 0, True)

    cur = qpos >> 6
    forced = (lane == 0) | (lane == cur) | (lane == cur - 1)
    anyblk = jnp.zeros((1, LANE), F32)
    for g in range(G_NSA):
        imp = p[g * HPG] + p[g * HPG + 1] + p[g * HPG + 2] + p[g * HPG + 3]
        score = jnp.where(lane <= cur, jnp.where(forced, FORCE_SCORE, imp), -1.0)
        sel = _topk_select_lanes(score, lane <= cur, NSA_TOPK)
        selm_ref[g] = (sel - 1.0).astype(BF16)
        anyblk = jnp.maximum(anyblk, jnp.max(sel, axis=0, keepdims=True))

    def finish(branch):
        acc = acc_ref[...]
        add_gated(acc[:, 0:LANE] / acc[:, LANE:2 * LANE], branch, False)

    m_ref[...] = jnp.full_like(m_ref, M_INIT)
    acc_ref[...] = jnp.zeros_like(acc_ref)
    kt_last = (q0 + tq - 1) // tk

    def tile(kt, causal):
        k0 = pl.multiple_of(kt * tk, tk)
        ka = jnp.concatenate([slc_ref[0, pl.ds(k0, tk), 0:LANE], kpos_ref[pl.ds(k0, tk), :]], axis=1)
        va = _with_ones(slc_ref[0, pl.ds(k0, tk), LANE:2 * LANE])
        ebt = ebt_ref[pl.ds(k0, tk), :]
        mbs = []
        for g in range(G_NSA):
            mb = _nt(selm_ref[g], ebt)
            if causal:
                kp = k0 + lax.broadcasted_iota(jnp.int32, (tq, tk), 1)
                qp = q0 + lax.broadcasted_iota(jnp.int32, (tq, tk), 0)
                mb = jnp.where(kp <= qp, mb, -BIG)
            mbs.append(mb)

        def bias_fn(c0):
            return jnp.tile(mbs[c0 // (HPG * tq)], (chunk // tq, 1))

        _flash_tile(qa_ref, ka, va, m_ref, acc_ref, chunk, bias_fn)

    bpt = tk // NSA_BLOCK
    blk1 = lax.broadcasted_iota(jnp.int32, (1, LANE), 1)

    def body(kt, carry):
        in_tile = (blk1 >= kt * bpt) & (blk1 < (kt + 1) * bpt)
        wanted = jnp.max(jnp.where(in_tile, anyblk, 0.0))

        @pl.when(wanted > 0.0)
        def _():
            tile(kt, False)

        return carry

    lax.fori_loop(0, kt_last, body, 0)
    tile(kt_last, True)
    finish(1)

    wk = NSA_WINDOW + tq
    kstart = pl.multiple_of(jnp.clip(q0 - NSA_WINDOW, 0, seq - wk), 8)
    ka = jnp.concatenate([win_ref[0, pl.ds(kstart, wk), 0:LANE], kpos_ref[pl.ds(kstart, wk), :]], axis=1)
    va = _with_ones(win_ref[0, pl.ds(kstart, wk), LANE:2 * LANE])
    dist = (q0 - kstart) + (lax.broadcasted_iota(jnp.int32, (tq, wk), 0)
                            - lax.broadcasted_iota(jnp.int32, (tq, wk), 1))
    wmask = jnp.where((dist >= 0) & (dist < NSA_WINDOW), 0.0, -BIG)
    m_ref[...] = jnp.full_like(m_ref, M_INIT)
    acc_ref[...] = jnp.zeros_like(acc_ref)
    _flash_tile(qa_ref, ka, va, m_ref, acc_ref, chunk, lambda c0: jnp.tile(wmask, (chunk // tq, 1)))
    finish(2)

    for j in range(HPG):
        lo = og_ref[j * tq:(j + 1) * tq, :]
        hi = og_ref[(HPG + j) * tq:(HPG + j + 1) * tq, :]
        o_ref[0, :, j * LANE:(j + 1) * LANE] = jnp.where(lane < HEAD_DIM, lo, hi)


def _nsa_prompt(qn, small, kvc, cpos, slcb, winb, kpos, ebt, tq, tk, chunk):
    b, seq, _ = qn.shape
    nblk = seq // NSA_BLOCK
    kern = functools.partial(_nsa_prompt_kernel, tq=tq, tk=tk, seq=seq, nblk=nblk, chunk=chunk)
    rows = H_NSA * tq
    return pl.pallas_call(
        kern,
        grid=(b, seq // tq),
        in_specs=[pl.BlockSpec((1, tq, W_Q), lambda i, j: (i, j, 0)),
                  pl.BlockSpec((1, tq, LANE), lambda i, j: (i, j, 0)),
                  pl.BlockSpec((1, LANE, 256), lambda i, j: (i, 0, 0)),
                  pl.BlockSpec((LANE, LANE), lambda i, j: (0, 0)),
                  pl.BlockSpec((1, seq, 256), lambda i, j: (i, 0, 0)),
                  pl.BlockSpec((1, seq, 256), lambda i, j: (i, 0, 0)),
                  pl.BlockSpec((seq, LANE), lambda i, j: (0, 0)),
                  pl.BlockSpec((seq, LANE), lambda i, j: (0, 0))],
        out_specs=pl.BlockSpec((1, tq, W_Q), lambda i, j: (i, j, 0)),
        out_shape=jax.ShapeDtypeStruct((b, seq, W_Q), F32),
        scratch_shapes=[pltpu.VMEM((rows, 2 * LANE), BF16),
                        pltpu.VMEM((G_NSA, tq, LANE), BF16),
                        pltpu.VMEM((rows, LANE), F32),
                        pltpu.VMEM((rows, 2 * LANE), F32),
                        pltpu.VMEM((rows, LANE), F32)],
        compiler_params=_cparams(("parallel", "parallel")),
        name="nsa_prompt",
    )(qn, small, kvc, cpos, slcb, winb, kpos, ebt)


def _fox_prep_kernel(lf_ref, ltri_ref, place_ref, o_ref, carry_ref):
    t = pl.program_id(1)

    @pl.when(t == 0)
    def _():
        carry_ref[...] = jnp.zeros_like(carry_ref)

    def split3(x):
        hi = x.astype(BF16)
        r = x - hi.astype(F32)
        mid = r.astype(BF16)
        return hi, mid, (r - mid.astype(F32)).astype(BF16)

    ltri = ltri_ref[...]
    c = carry_ref[...]
    for part in split3(lf_ref[0]):
        c = c + jnp.dot(ltri, part, preferred_element_type=F32)
    tc = c.shape[0]
    carry_ref[...] = c[tc - 1:tc, :]
    out = jnp.zeros((tc, LANE), F32)
    for i, part in enumerate(split3(c)):
        out = out + jnp.dot(part, place_ref[i], preferred_element_type=F32)
    o_ref[0] = out.astype(BF16)


def _fox_prep(lf, ltri, place, tc):
    b, n, _ = lf.shape
    return pl.pallas_call(
        _fox_prep_kernel,
        grid=(b, n // tc),
        in_specs=[pl.BlockSpec((1, tc, LANE), lambda i, t: (i, t, 0)),
                  pl.BlockSpec((tc, tc), lambda i, t: (0, 0)),
                  pl.BlockSpec((3, LANE, LANE), lambda i, t: (0, 0, 0))],
        out_specs=pl.BlockSpec((1, tc, LANE), lambda i, t: (i, t, 0)),
        out_shape=jax.ShapeDtypeStruct((b, n, LANE), BF16),
        scratch_shapes=[pltpu.VMEM((1, LANE), F32)],
        compiler_params=_cparams(("parallel", "arbitrary")),
        name="fox_prep",
    )(lf, ltri, place)


def _qf_aug(q, pair, tq):
    lane = lax.broadcasted_iota(jnp.int32, (tq, LANE), 1)
    q = q.astype(F32)
    parts = []
    for e in range(2):
        qm = jnp.where((lane >> 6) == e, q, 0.0)
        d = lane - (8 * pair + e)
        aug = jnp.where(d == 0, -1.0, jnp.where(d == 2, -1.0, jnp.where(d == 4, -1.0, 0.0)))
        parts.append(jnp.concatenate([qm, aug], axis=1))
    return jnp.concatenate(parts, axis=0)


def _fox_prompt_kernel(q_ref, k_ref, v_ref, ca_ref, o_ref, qa_ref, m_ref, acc_ref, *, tq, tk, chunk):
    pair = pl.program_id(1)
    qi = pl.program_id(2)
    qa_ref[...] = _qf_aug(q_ref[0], pair, tq).astype(BF16)
    m_ref[...] = jnp.full_like(m_ref, M_INIT)
    acc_ref[...] = jnp.zeros_like(acc_ref)
    ndiag = tq // tk

    def tile(kt, diag):
        k0 = pl.multiple_of(kt * tk, tk)
        ka = jnp.concatenate([k_ref[0, pl.ds(k0, tk), :], ca_ref[0, pl.ds(k0, tk), :]], axis=1)
        va = _with_ones(v_ref[0, pl.ds(k0, tk), :])
        bias_fn = None
        if diag is not None:
            def bias_fn(c0):
                r0 = c0 % tq
                if r0 + chunk - 1 < diag * tk:
                    return _SKIP
                if r0 >= diag * tk + tk - 1:
                    return None
                qrow = r0 + lax.broadcasted_iota(jnp.int32, (chunk, tk), 0)
                kcol = diag * tk + lax.broadcasted_iota(jnp.int32, (chunk, tk), 1)
                return jnp.where(kcol <= qrow, 0.0, -BIG)
        _flash_tile(qa_ref, ka, va, m_ref, acc_ref, chunk, bias_fn)

    def body(kt, carry):
        tile(kt, None)
        return carry

    lax.fori_loop(0, qi * ndiag, body, 0)
    for dg in range(ndiag):
        tile(qi * ndiag + dg, dg)
    acc = acc_ref[...]
    o = (acc[:, 0:LANE] / acc[:, LANE:2 * LANE]).reshape(2, tq, LANE)
    lane = lax.broadcasted_iota(jnp.int32, (tq, LANE), 1)
    o_ref[0] = jnp.where(lane < HEAD_DIM, o[0], o[1])


def _fox_prompt(qf, kvfb, caug, tq, tk, chunk):
    b, seq, _ = qf.shape
    npair = H_FOX // 2
    return pl.pallas_call(
        functools.partial(_fox_prompt_kernel, tq=tq, tk=tk, chunk=min(chunk, tq)),
        grid=(b, npair, seq // tq),
        in_specs=[pl.BlockSpec((1, tq, LANE), lambda i, p, j: (i, j, p)),
                  pl.BlockSpec((1, seq, LANE), lambda i, p, j: (i, 0, p)),
                  pl.BlockSpec((1, seq, LANE), lambda i, p, j: (i, 0, npair + p)),
                  pl.BlockSpec((1, seq, LANE), lambda i, p, j: (i, 0, 0))],
        out_specs=pl.BlockSpec((1, tq, LANE), lambda i, p, j: (i, j, p)),
        out_shape=jax.ShapeDtypeStruct((b, seq, W_QF), F32),
        scratch_shapes=[pltpu.VMEM((2 * tq, 2 * LANE), BF16),
                        pltpu.VMEM((2 * tq, LANE), F32),
                        pltpu.VMEM((2 * tq, 2 * LANE), F32)],
        compiler_params=_cparams(("parallel", "parallel", "parallel")),
        name="fox_prompt",
    )(qf, kvfb, kvfb, caug)


def _mixout_kernel(on_ref, of_ref, x_ref, gg_ref, w_ref, gp_ref, o_ref):
    gg = gg_ref[...]
    half = on_ref.shape[1]
    yn = _rms(on_ref[...], gg[:, :half])
    yf = _rms(of_ref[...], gg[:, half:])
    y = jnp.dot(jnp.concatenate([yn, yf], axis=1).astype(BF16), w_ref[...], preferred_element_type=F32)
    o_ref[...] = x_ref[...] + _rms(y, gp_ref[...])


def _mixout(on, of, x, gg, w, gp, tm):
    m, d = x.shape
    half = on.shape[1]
    return pl.pallas_call(
        _mixout_kernel,
        grid=(m // tm,),
        in_specs=[pl.BlockSpec((tm, half), lambda i: (i, 0)),
                  pl.BlockSpec((tm, half), lambda i: (i, 0)),
                  pl.BlockSpec((tm, d), lambda i: (i, 0)),
                  pl.BlockSpec((1, 2 * half), lambda i: (0, 0)),
                  pl.BlockSpec((2 * half, d), lambda i: (0, 0)),
                  pl.BlockSpec((1, d), lambda i: (0, 0))],
        out_specs=pl.BlockSpec((tm, d), lambda i: (i, 0)),
        out_shape=jax.ShapeDtypeStruct((m, d), F32),
        compiler_params=_cparams(("parallel",)),
        name="mixout",
    )(on, of, x, gg, w, gp)


def _gelu_tanh(x):
    return 0.5 * x * (1.0 + jnp.tanh(np.sqrt(2.0 / np.pi) * (x + 0.044715 * (x * x * x))))


def _ffn_kernel(*refs, tm, seq, has_state, tail):
    if has_state:
        (x_ref, xh_ref, st_ref, g_ref, wa_ref, wb_ref, cw_ref, cb_ref, wo_ref, gp_ref,
         o_ref, a_ref, h_ref, acc_ref) = refs
    else:
        (x_ref, xh_ref, g_ref, wa_ref, wb_ref, cw_ref, cb_ref, wo_ref, gp_ref,
         o_ref, a_ref, h_ref, acc_ref) = refs
    i = pl.program_id(0)
    j = pl.program_id(1)

    @pl.when(j == 0)
    def _():
        g = g_ref[...]
        h_ref[0:8, :] = _rms(xh_ref[...], g).astype(BF16)
        h_ref[8:, :] = _rms(x_ref[...], g).astype(BF16)
        acc_ref[...] = jnp.zeros_like(acc_ref)

    a_ext = jnp.dot(h_ref[...], wa_ref[...], preferred_element_type=F32)
    rows = lax.broadcasted_iota(jnp.int32, a_ext.shape, 0)
    if has_state:
        a_ext = jnp.where((rows & 7) >= 6, st_ref[...], a_ext)
    else:
        keep = jnp.logical_or(rows >= 8, (i * tm) % seq != 0)
        a_ext = jnp.where(keep, a_ext, 0.0)
    a = a_ext[8:, :]
    a1 = pltpu.roll(a_ext, 1, 0)[8:, :]
    a2 = pltpu.roll(a_ext, 2, 0)[8:, :]
    cw = cw_ref[...]
    ac = cb_ref[...] + cw[0:1, :] * a2 + cw[1:2, :] * a1 + cw[2:3, :] * a
    b = jnp.dot(h_ref[8:, :], wb_ref[...], preferred_element_type=F32)
    acc_ref[...] += jnp.dot((_gelu_tanh(ac) * b).astype(BF16), wo_ref[...], preferred_element_type=F32)
    a_ref[0] = a[tm - tail:, :]

    @pl.when(j == pl.num_programs(1) - 1)
    def _():
        o_ref[...] = x_ref[...] + _rms(acc_ref[...], gp_ref[...])


def _ffn(x, state_ext, g, wa, wb, cw, cb, wo, gp, tm, tf, seq, tail):
    m, d = x.shape
    dff = wa.shape[1]
    has_state = state_ext is not None
    nb = m // tm
    halo = lambda i, j: (jnp.maximum(i * (tm // 8) - 1, 0), 0)
    in_specs = [pl.BlockSpec((tm, d), lambda i, j: (i, 0)), pl.BlockSpec((8, d), halo)]
    args = [x, x]
    if has_state:
        in_specs.append(pl.BlockSpec((tm + 8, tf), lambda i, j: (0, j)))
        args.append(state_ext)
    in_specs += [pl.BlockSpec((1, d), lambda i, j: (0, 0)),
                 pl.BlockSpec((d, tf), lambda i, j: (0, j)),
                 pl.BlockSpec((d, tf), lambda i, j: (0, j)),
                 pl.BlockSpec((CONV_W, tf), lambda i, j: (0, j)),
                 pl.BlockSpec((1, tf), lambda i, j: (0, j)),
                 pl.BlockSpec((tf, d), lambda i, j: (j, 0)),
                 pl.BlockSpec((1, d), lambda i, j: (0, 0))]
    args += [g, wa, wb, cw, cb, wo, gp]
    return pl.pallas_call(
        functools.partial(_ffn_kernel, tm=tm, seq=seq, has_state=has_state, tail=tail),
        grid=(nb, dff // tf),
        in_specs=in_specs,
        out_specs=[pl.BlockSpec((tm, d), lambda i, j: (i, 0)),
                   pl.BlockSpec((1, tail, tf), lambda i, j: (i, 0, j))],
        out_shape=[jax.ShapeDtypeStruct((m, d), F32), jax.ShapeDtypeStruct((nb, tail, dff), F32)],
        scratch_shapes=[pltpu.VMEM((tm + 8, d), BF16), pltpu.VMEM((tm, d), F32)],
        compiler_params=_cparams(("parallel", "arbitrary")),
        name="ffn",
    )(*args)


def _page_copy(cache_ref, layer, page, r0, nrows, dst, sem):
    return pltpu.make_async_copy(cache_ref.at[layer, page, pl.ds(r0, nrows), :], dst, sem)


def _stream_pages(step, nsteps, n_pages, copies_fn):
    slot = step % 2

    def start_all(s, sl):
        def body(p, c):
            for i, cp in enumerate(copies_fn(s, sl, p)):
                cp.start(priority=i % 2)
            return c
        lax.fori_loop(0, n_pages, body, 0)

    @pl.when(step == 0)
    def _():
        start_all(step, slot)

    @pl.when(step + 1 < nsteps)
    def _():
        start_all(step + 1, 1 - slot)

    def wait_all():
        def body(p, c):
            for cp in copies_fn(step, slot, p):
                cp.wait()
            return c
        lax.fori_loop(0, n_pages, body, 0)

    return slot, wait_all


def _cmp_sample_kernel(pt_ref, cn_ref, pet_ref, w_ref, o_ref, cbuf, sem, *, layer, n_pages):
    def copies(s, sl, p):
        return (_page_copy(cn_ref, layer, pt_ref[s, p], 0, 256, cbuf.at[sl, :, p, :], sem.at[sl]),)

    slot, wait_all = _stream_pages(pl.program_id(0), pl.num_programs(0), n_pages, copies)
    wait_all()

    for c in range(2):
        def lhs_of(d):
            pe = pet_ref[pl.ds(d, 1), :]
            rows = [cbuf[slot, c * 128 + g * 64 + d] for g in range(G_NSA)]
            return (jnp.concatenate(rows, axis=0) + pe).astype(BF16)

        def body(d2, acc):
            lhs = jnp.concatenate([lhs_of(2 * d2), lhs_of(2 * d2 + 1)], axis=1)
            return acc + jnp.dot(lhs, w_ref[c, d2], preferred_element_type=F32)

        acc = lax.fori_loop(0, HEAD_DIM // 2, body, jnp.zeros((G_NSA * n_pages, LANE), F32), unroll=4)
        o_ref[0, c] = acc.astype(BF16)


def _cmp_sample(page_table, cn, pet, wcd, layer):
    db, n_pages = page_table.shape
    return pl.pallas_call(
        functools.partial(_cmp_sample_kernel, layer=layer, n_pages=n_pages),
        grid_spec=pltpu.PrefetchScalarGridSpec(
            num_scalar_prefetch=1, grid=(db,),
            in_specs=[pl.BlockSpec(memory_space=pl.ANY),
                      pl.BlockSpec((HEAD_DIM, LANE), lambda i, pt: (0, 0)),
                      pl.BlockSpec((2, HEAD_DIM // 2, 2 * LANE, LANE), lambda i, pt: (0, 0, 0, 0))],
            out_specs=pl.BlockSpec((1, 2, G_NSA * n_pages, LANE), lambda i, pt: (i, 0, 0, 0)),
            scratch_shapes=[pltpu.VMEM((2, 256, n_pages, LANE), F32), pltpu.SemaphoreType.DMA((2,))]),
        out_shape=jax.ShapeDtypeStruct((db, 2, G_NSA * n_pages, LANE), BF16),
        compiler_params=_cparams(("arbitrary",)),
        name="cmp_sample",
    )(page_table, cn, pet, wcd)


def _softmax2(parts):
    m = functools.reduce(jnp.maximum, [jnp.max(s, axis=-1, keepdims=True) for s in parts])
    es = [jnp.exp(s - m) for s in parts]
    d = functools.reduce(jnp.add, [jnp.sum(e, axis=-1, keepdims=True) for e in es])
    return es, d


def _pad_rows(x, n):
    return jnp.concatenate([x, jnp.zeros((n - x.shape[0],) + x.shape[1:], x.dtype)], axis=0)


def _nsa_sample_kernel(pt_ref, q_ref, gate_ref, kc_ref, cn_ref, wst_ref, slcn_ref, winn_ref, eb_ref, o_ref,
                       kbuf, vbuf, sem, *, layer, n_pages, past, dseq, rp):
    nh = H_NSA
    r = nh * rp

    def copies(s, sl, p):
        page = pt_ref[s, p]
        win = pl.ds(pl.multiple_of(p * LANE, LANE), LANE)
        return (_page_copy(cn_ref, layer, page, 256, 128, kbuf.at[sl, :, win], sem.at[sl]),
                _page_copy(cn_ref, layer, page, 384, 128, vbuf.at[sl, :, win], sem.at[sl]))

    slot, wait_all = _stream_pages(pl.program_id(0), pl.num_programs(0), n_pages, copies)

    lane = lax.broadcasted_iota(jnp.int32, (rp, LANE), 1)
    row = lax.broadcasted_iota(jnp.int32, (rp, LANE), 0)
    tok = jnp.minimum(row, dseq - 1)
    rr = lax.broadcasted_iota(jnp.int32, (r, 1), 0)
    hrow = rr >> 3
    grow = hrow >> 2
    slope = lax.bitcast_convert_type((126 - hrow) << 23, F32)
    qpos_r = past + jnp.minimum(rr & 7, dseq - 1)

    qs, qsw = [], []
    for g in range(G_NSA):
        for j in range(HPG):
            q32 = q_ref[0, :, j * LANE:(j + 1) * LANE].astype(F32)
            qm = jnp.where((lane >> 6) == g, q32, 0.0)
            qs.append(qm)
            qsw.append(pltpu.roll(qm, HEAD_DIM, 1))
    qm_all = jnp.concatenate(qs, axis=0)
    qsw_all = jnp.concatenate(qsw, axis=0)
    q_lo = jnp.where(grow == 0, qm_all, qsw_all).astype(BF16)
    q_hi = jnp.where(grow == 0, qsw_all, qm_all).astype(BF16)
    qm_bf = qm_all.astype(BF16)

    lane_r = lax.broadcasted_iota(jnp.int32, (r, LANE), 1)
    own = (lane_r >> 6) == grow
    kc = kc_ref[0, 0]
    vc = kc_ref[0, 1]
    s_par, vis_par = [], []
    for par, qq in ((0, q_lo), (1, q_hi)):
        blk = 2 * (lane_r & 63) + par
        dist = qpos_r - (blk * NSA_BLOCK + (NSA_BLOCK - 1))
        s_par.append(_nt(qq, kc) - slope * dist.astype(F32))
        vis_par.append(own & (dist >= 0) & ((lane_r & 63) < n_pages))
    sm = [jnp.where(v, s, NEG) for s, v in zip(s_par, vis_par)]
    mx = jnp.maximum(jnp.max(sm[0], axis=-1, keepdims=True), jnp.max(sm[1], axis=-1, keepdims=True))
    es = [jnp.where(v, jnp.exp(s - mx), 0.0) for s, v in zip(sm, vis_par)]
    den = jnp.maximum(jnp.sum(es[0], axis=-1, keepdims=True) + jnp.sum(es[1], axis=-1, keepdims=True), 1e-30)
    ps = [e / den for e in es]
    a0 = jnp.dot(ps[0].astype(BF16), vc, preferred_element_type=F32)
    a1 = jnp.dot(ps[1].astype(BF16), vc, preferred_element_type=F32)
    cmix = jnp.where(lane_r < HEAD_DIM, a0, a1)
    o_cmp = cmix + pltpu.roll(cmix, HEAD_DIM, 1)

    cur = past // NSA_BLOCK
    selms = []
    for g in range(G_NSA):
        sc, ids, oks = [], [], []
        for par in range(2):
            p4 = ps[par].reshape(nh, rp, LANE)
            imp = p4[g * HPG] + p4[g * HPG + 1] + p4[g * HPG + 2] + p4[g * HPG + 3]
            blk = 2 * (lane & 63) + par
            ok = ((lane >> 6) == g) & ((lane & 63) < n_pages)
            forced = (blk == 0) | (blk == cur - 1)
            sc.append(jnp.where(ok, jnp.where(forced, FORCE_SCORE, imp), -2.0))
            ids.append(blk + jnp.where(ok, 0, 1 << 20) + (lane >> 6) * (1 << 10))
            oks.append(ok)
        sel = _topk_select(jnp.concatenate(sc, axis=1), jnp.concatenate(ids, axis=1), NSA_TOPK - 1)
        selms.append(jnp.where(jnp.concatenate(oks, axis=1), sel - 1.0, 0.0))
    selm = jnp.concatenate(selms, axis=0).astype(BF16)

    wait_all()
    nk = n_pages * LANE
    s_p = jnp.dot(qm_bf, kbuf[slot].astype(BF16), preferred_element_type=F32)
    kpos = lax.broadcasted_iota(jnp.int32, (r, nk), 1)
    mb = jnp.dot(selm, eb_ref[...], preferred_element_type=F32)
    mb = jnp.broadcast_to(mb.reshape(G_NSA, 1, rp, nk), (G_NSA, HPG, rp, nk)).reshape(r, nk)
    s_p = s_p - slope * (qpos_r - kpos).astype(F32) + mb
    kn = _pad_rows(slcn_ref[0, :, 0:LANE], LANE)
    vn = _pad_rows(slcn_ref[0, :, LANE:2 * LANE], LANE)
    s_n = _nt(qm_bf, kn)
    kposn = past + lane_r
    okn = (lane_r < dseq) & (kposn <= qpos_r)
    s_n = jnp.where(okn, s_n - slope * (qpos_r - kposn).astype(F32), -BIG)
    (e_p, e_n), den = _softmax2([s_p, s_n])
    o_slc = (_nt(e_p.astype(BF16), vbuf[slot].astype(BF16))
             + jnp.dot(e_n.astype(BF16), vn, preferred_element_type=F32)) / den

    wbs = wst_ref.shape[3]
    s_w = jnp.dot(qm_bf, wst_ref[0, 0, 0:LANE, :].astype(BF16), preferred_element_type=F32)
    kposw = (past - wbs) + lax.broadcasted_iota(jnp.int32, (r, wbs), 1)
    dw = qpos_r - kposw
    s_w = jnp.where((dw >= 0) & (dw < NSA_WINDOW) & (kposw >= 0), s_w - slope * dw.astype(F32), -BIG)
    kwn = _pad_rows(winn_ref[0, :, 0:LANE], LANE)
    vwn = _pad_rows(winn_ref[0, :, LANE:2 * LANE], LANE)
    dn = qpos_r - kposn
    s_wn = jnp.where((lane_r < dseq) & (dn >= 0) & (dn < NSA_WINDOW), _nt(qm_bf, kwn) - slope * dn.astype(F32), -BIG)
    (e_w, e_wn), denw = _softmax2([s_w, s_wn])
    o_win = (_nt(e_w.astype(BF16), wst_ref[0, 0, LANE:2 * LANE, :].astype(BF16))
             + jnp.dot(e_wn.astype(BF16), vwn, preferred_element_type=F32)) / denw

    gate = gate_ref[0]
    o_cmp = o_cmp.reshape(nh, rp, LANE)
    o_slc = o_slc.reshape(nh, rp, LANE)
    o_win = o_win.reshape(nh, rp, LANE)
    for j in range(HPG):
        parts = []
        for g in range(G_NSA):
            hh = g * HPG + j
            c = 2 * j + g
            parts.append(gate[:, c:c + 1] * o_cmp[hh] + gate[:, 8 + c:9 + c] * o_slc[hh]
                         + gate[:, 16 + c:17 + c] * o_win[hh])
        o_ref[0, :, j * LANE:(j + 1) * LANE] = jnp.where(lane < HEAD_DIM, parts[0], parts[1])


def _nsa_sample(page_table, qn, small, kcs, cn, wst, slcn, winn, eb, layer, past, dseq):
    db, n_pages = page_table.shape
    rp = qn.shape[1]
    nk = n_pages * LANE
    wbs = wst.shape[3]
    kern = functools.partial(_nsa_sample_kernel, layer=layer, n_pages=n_pages, past=past, dseq=dseq, rp=rp)
    return pl.pallas_call(
        kern,
        grid_spec=pltpu.PrefetchScalarGridSpec(
            num_scalar_prefetch=1, grid=(db,),
            in_specs=[pl.BlockSpec((1, rp, W_Q), lambda i, pt: (i, 0, 0)),
                      pl.BlockSpec((1, rp, LANE), lambda i, pt: (i, 0, 0)),
                      pl.BlockSpec((1, 2, LANE, LANE), lambda i, pt: (i, 0, 0, 0)),
                      pl.BlockSpec(memory_space=pl.ANY),
                      pl.BlockSpec((1, 1, 2 * LANE, wbs), lambda i, pt: (layer, i, 0, 0)),
                      pl.BlockSpec((1, rp, 2 * LANE), lambda i, pt: (i, 0, 0)),
                      pl.BlockSpec((1, rp, 2 * LANE), lambda i, pt: (i, 0, 0)),
                      pl.BlockSpec((2 * LANE, nk), lambda i, pt: (0, 0))],
            out_specs=pl.BlockSpec((1, rp, W_Q), lambda i, pt: (i, 0, 0)),
            scratch_shapes=[pltpu.VMEM((2, LANE, nk), F32), pltpu.VMEM((2, LANE, nk), F32),
                            pltpu.SemaphoreType.DMA((2,))]),
        out_shape=jax.ShapeDtypeStruct((db, rp, W_Q), F32),
        compiler_params=_cparams(("arbitrary",)),
        name="nsa_sample",
    )(page_table, qn, small, kcs, cn, wst, slcn, winn, eb)


def _logf_sample_kernel(pt_ref, cl_ref, utri_ref, sl_ref, o_ref, lbuf, sem, *, layer, n_pages):
    def copies(s, sl, p):
        dst = lbuf.at[sl, pl.ds(p * H_FOX, H_FOX), :]
        return (pltpu.make_async_copy(cl_ref.at[layer, pt_ref[s, p]], dst, sem.at[sl]),)

    slot, wait_all = _stream_pages(pl.program_id(0), pl.num_programs(0), n_pages, copies)
    wait_all()

    def split3(x):
        hi = x.astype(BF16)
        rem = x - hi.astype(F32)
        mid = rem.astype(BF16)
        return hi, mid, (rem - mid.astype(F32)).astype(BF16)

    utri = utri_ref[...]
    lf = lbuf[slot]
    w = jnp.zeros(lf.shape, F32)
    for part in split3(lf):
        w = w + jnp.dot(part, utri, preferred_element_type=F32)
    tot = jnp.broadcast_to(w[:, LANE - 1:LANE], w.shape)
    carry = jnp.zeros(lf.shape, F32)
    sl = sl_ref[...]
    for part in split3(tot):
        carry = carry + jnp.dot(sl, part, preferred_element_type=F32)
    o_ref[0] = w + carry


def _logf_sample(page_table, cl, utri, slm, layer):
    db, n_pages = page_table.shape
    rows = n_pages * H_FOX
    return pl.pallas_call(
        functools.partial(_logf_sample_kernel, layer=layer, n_pages=n_pages),
        grid_spec=pltpu.PrefetchScalarGridSpec(
            num_scalar_prefetch=1, grid=(db,),
            in_specs=[pl.BlockSpec(memory_space=pl.ANY),
                      pl.BlockSpec((LANE, LANE), lambda i, pt: (0, 0)),
                      pl.BlockSpec((rows, rows), lambda i, pt: (0, 0))],
            out_specs=pl.BlockSpec((1, rows, LANE), lambda i, pt: (i, 0, 0)),
            scratch_shapes=[pltpu.VMEM((2, rows, LANE), F32), pltpu.SemaphoreType.DMA((2,))]),
        out_shape=jax.ShapeDtypeStruct((db, rows, LANE), F32),
        compiler_params=_cparams(("arbitrary",)),
        name="logf_sample",
    )(page_table, cl, utri, slm)


def _fox_sample_kernel(pt_ref, q_ref, c_ref, clast_ref, lfn_ref, kn_ref, vn_ref, place_ref, cf_ref, o_ref,
                       kbuf, vbuf, sem, *, layer, n_pages, dseq, rp):
    b = pl.program_id(0)
    pair = pl.program_id(1)
    npair = pl.num_programs(1)
    half = H_FOX * HEAD_DIM

    def copies(s, sl, p):
        page = pt_ref[s // npair, p]
        win = pl.ds(pl.multiple_of(p * LANE, LANE), LANE)
        r0 = pl.multiple_of((s % npair) * LANE, LANE)
        return (_page_copy(cf_ref, layer, page, r0, LANE, kbuf.at[sl, :, win], sem.at[sl]),
                _page_copy(cf_ref, layer, page, half + r0, LANE, vbuf.at[sl, :, win], sem.at[sl]))

    slot, wait_all = _stream_pages(b * npair + pair, pl.num_programs(0) * npair, n_pages, copies)

    qa = _qf_aug(q_ref[0], pair, rp).astype(BF16)
    r = 2 * rp
    nk = n_pages * LANE

    rows8 = lax.broadcasted_iota(jnp.int32, (rp, LANE), 0)
    lfn = lfn_ref[0]
    cn = jnp.broadcast_to(clast_ref[0], (rp, LANE))
    for k in range(dseq):
        cn = cn + jnp.where(rows8 >= k, lfn[k:k + 1, :], 0.0)
    hi = cn.astype(BF16)
    rem = cn - hi.astype(F32)
    mid = rem.astype(BF16)
    lo = (rem - mid.astype(F32)).astype(BF16)
    caug = (jnp.dot(hi, place_ref[0], preferred_element_type=F32) + jnp.dot(mid, place_ref[1], preferred_element_type=F32)
            + jnp.dot(lo, place_ref[2], preferred_element_type=F32)).astype(BF16)
    ka_n = _pad_rows(jnp.concatenate([kn_ref[0], caug], axis=1), LANE)
    s_n = _nt(qa, ka_n)
    lane_r = lax.broadcasted_iota(jnp.int32, (r, LANE), 1)
    tok_r = jnp.minimum(lax.broadcasted_iota(jnp.int32, (r, LANE), 0) & (rp - 1), dseq - 1)
    s_n = jnp.where((lane_r < dseq) & (lane_r <= tok_r), s_n, -BIG)

    wait_all()
    s_p = jnp.dot(qa[:, 0:LANE], kbuf[slot].astype(BF16), preferred_element_type=F32)
    s_p = (s_p.reshape(2, rp, nk) - c_ref[0, 0][:, None, :]).reshape(r, nk)
    (e_p, e_n), den = _softmax2([s_p, s_n])
    o = (_nt(e_p.astype(BF16), vbuf[slot].astype(BF16))
         + jnp.dot(e_n.astype(BF16), _pad_rows(vn_ref[0], LANE), preferred_element_type=F32)) / den
    o = o.reshape(2, rp, LANE)
    lane = lax.broadcasted_iota(jnp.int32, (rp, LANE), 1)
    o_ref[0] = jnp.where(lane < HEAD_DIM, o[0], o[1])


def _fox_sample(page_table, qf, cflat, clast, small, kvfb, place, cf, layer, dseq):
    db, n_pages = page_table.shape
    rp = qf.shape[1]
    npair = H_FOX // 2
    nk = n_pages * LANE
    kern = functools.partial(_fox_sample_kernel, layer=layer, n_pages=n_pages, dseq=dseq, rp=rp)
    return pl.pallas_call(
        kern,
        grid_spec=pltpu.PrefetchScalarGridSpec(
            num_scalar_prefetch=1, grid=(db, npair),
            in_specs=[pl.BlockSpec((1, rp, LANE), lambda i, p, pt: (i, 0, p)),
                      pl.BlockSpec((1, 1, 2, nk), lambda i, p, pt: (i, p, 0, 0)),
                      pl.BlockSpec((1, 1, LANE), lambda i, p, pt: (i, 0, 0)),
                      pl.BlockSpec((1, rp, LANE), lambda i, p, pt: (i, 0, 0)),
                      pl.BlockSpec((1, rp, LANE), lambda i, p, pt: (i, 0, p)),
                      pl.BlockSpec((1, rp, LANE), lambda i, p, pt: (i, 0, npair + p)),
                      pl.BlockSpec((3, LANE, LANE), lambda i, p, pt: (0, 0, 0)),
                      pl.BlockSpec(memory_space=pl.ANY)],
            out_specs=pl.BlockSpec((1, rp, LANE), lambda i, p, pt: (i, 0, p)),
            scratch_shapes=[pltpu.VMEM((2, LANE, nk), F32), pltpu.VMEM((2, LANE, nk), F32),
                            pltpu.SemaphoreType.DMA((2,))]),
        out_shape=jax.ShapeDtypeStruct((db, rp, W_QF), F32),
        compiler_params=_cparams(("arbitrary", "arbitrary")),
        name="fox_sample",
    )(page_table, qf, cflat, clast, small, kvfb, kvfb, place, cf)


def _perm_heads(x, axis):
    shp = x.shape
    x = x.reshape(shp[:axis] + (G_NSA, HPG) + shp[axis + 1:])
    x = jnp.swapaxes(x, axis, axis + 1)
    return x.reshape(shp)


def _prep_layer_weights(w_in, b_gate, b_forget, cmp_pe, w_cmp_k, w_cmp_v, g_group, w_out):
    d = w_in.shape[0]
    o = 0
    qn = w_in[:, o:o + W_Q]; o += W_Q
    kvn = w_in[:, o:o + W_KVN]; o += W_KVN
    gate = w_in[:, o:o + W_GATE]; o += W_GATE
    qf = w_in[:, o:o + W_QF]; o += W_QF
    kvf = w_in[:, o:o + W_KVF]; o += W_KVF
    fg = w_in[:, o:o + H_FOX]
    qn = _perm_heads(qn.reshape(d, H_NSA, HEAD_DIM), 1).reshape(d, W_Q)
    gate = _perm_heads(gate.reshape(d, 3, H_NSA), 2).reshape(d, W_GATE)
    small = jnp.concatenate([gate, fg, jnp.zeros((d, LANE - W_GATE - H_FOX), w_in.dtype)], axis=1)
    w_proj = jnp.concatenate([qn, kvn, qf, kvf, small], axis=1).astype(BF16)
    brow = jnp.concatenate([_perm_heads(b_gate.reshape(3, H_NSA), 1).reshape(W_GATE), b_forget,
                            jnp.zeros((LANE - W_GATE - H_FOX,), F32)]).reshape(1, LANE)
    w4 = jnp.stack([w_cmp_k, w_cmp_k, w_cmp_v, w_cmp_v])
    w_big = jnp.einsum("xc,clde->lxdce", jnp.eye(4, dtype=F32), w4)
    w_big = w_big.reshape(NSA_BLOCK * 4 * HEAD_DIM, 4 * HEAD_DIM).astype(BF16)
    pe_row = jnp.broadcast_to(cmp_pe[:, None, :], (NSA_BLOCK, 4, HEAD_DIM))
    pe_rows = jnp.zeros((8, NSA_BLOCK * 4 * HEAD_DIM), F32).at[0].set(pe_row.reshape(-1)).astype(BF16)
    gg = jnp.concatenate([_perm_heads(g_group[:W_Q].reshape(H_NSA, HEAD_DIM), 0).reshape(W_Q),
                          g_group[W_Q:]]).reshape(1, -1)
    w_o = jnp.concatenate([_perm_heads(w_out[:W_Q].reshape(H_NSA, HEAD_DIM, -1), 0).reshape(W_Q, -1),
                           w_out[W_Q:]], axis=0).astype(BF16)
    return w_proj, brow, w_big, pe_rows, gg, w_o


def _pos_aug(pos):
    n = pos.shape[0]
    z = jnp.zeros((n, LANE), F32)
    z = z.at[:, 0].set((pos // NSA_BLOCK).astype(F32)).at[:, 1].set((pos % NSA_BLOCK).astype(F32))
    return z.at[:, 2].set(1.0).at[:, 3].set(1.0).astype(BF16)


def _expand_t(nkeys, nblk_pad):
    kb = jnp.arange(nkeys)[:, None] // NSA_BLOCK
    return jnp.where(kb == jnp.arange(nblk_pad)[None, :], BIG, 0.0).astype(BF16)


def _fox_consts(tc):
    ltri = (jnp.arange(tc)[:, None] >= jnp.arange(tc)[None, :]).astype(BF16)
    place = np.zeros((3, LANE, LANE), np.float32)
    for i in range(3):
        for h in range(H_FOX):
            place[i, LOGF_LANE + h, 8 * (h // 2) + 2 * i + (h % 2)] = 1.0
    return ltri, jnp.asarray(place, BF16)


def _mixer_prompt(x, lw, consts, b, seq, tiles):
    w_proj, brow, w_big, pe_rows, gg, w_o, g_pre, g_post = lw
    m = b * seq
    qn, nsakv, winf, cmpb, slcb, winb, qf, kvf, kvfb, small = _proj(x, g_pre, w_proj, brow, tiles["tm"], seq)
    nblk = seq // NSA_BLOCK
    nb = b * nblk
    kvc = _compress(cmpb.reshape(nb, NSA_BLOCK * 256), pe_rows, w_big, min(256, nb), 4096).reshape(b, nblk, 256)
    if nblk < LANE:
        kvc = jnp.pad(kvc, ((0, 0), (0, LANE - nblk), (0, 0)))
    o_n = _nsa_prompt(qn.reshape(b, seq, W_Q), small.reshape(b, seq, LANE), kvc, consts["cpos"],
                      slcb.reshape(b, seq, 256), winb.reshape(b, seq, 256), consts["kpos"], consts["ebt"],
                      tiles["nsa_tq"], tiles["nsa_tk"], tiles["nsa_chunk"])
    caug = _fox_prep(small.reshape(b, seq, LANE), consts["ltri"], consts["place"], tiles["tc"])
    o_f = _fox_prompt(qf.reshape(b, seq, W_QF), kvfb.reshape(b, seq, W_KVF), caug, tiles["fox_tq"],
                      tiles["fox_tk"], tiles["fox_chunk"])
    y = _mixout(o_n.reshape(m, W_Q), o_f.reshape(m, W_QF), x, gg, w_o, g_post, tiles["tm"])
    return y, nsakv, winf, kvf, small


RP = 8


def _sample_consts(n_pages, w_cmp_k, w_cmp_v, cmp_pe):
    w2 = jnp.stack([w_cmp_k, w_cmp_v])
    wcd = jnp.einsum("xy,clde->cdxlye", jnp.eye(2, dtype=F32), w2)
    wcd = wcd.reshape(2, HEAD_DIM // 2, 4 * NSA_BLOCK, 2 * HEAD_DIM).astype(BF16)
    pet = jnp.tile(cmp_pe.T, (1, 2))
    return wcd, pet


def _expand_sample(n_pages):
    rowi = jnp.arange(2 * LANE)
    blk = 2 * (rowi % 64) + rowi // LANE
    ok = (rowi % 64) < n_pages
    kb = jnp.arange(n_pages * LANE) // NSA_BLOCK
    return jnp.where((blk[:, None] == kb[None, :]) & ok[:, None], BIG, 0.0).astype(BF16)


def _logf_consts(n_pages):
    utri = (jnp.arange(LANE)[:, None] <= jnp.arange(LANE)[None, :]).astype(BF16)
    ri = jnp.arange(n_pages * H_FOX)
    slm = ((ri[:, None] % H_FOX == ri[None, :] % H_FOX) & (ri[None, :] // H_FOX < ri[:, None] // H_FOX)).astype(BF16)
    return utri, slm


def _mixer_sample(x, lw, sc, views, page_table, layer, past, dseq):
    w_proj, brow, _, _, gg, w_o, g_pre, g_post = lw
    cn, cf, cl, wst = views
    db, n_pages = page_table.shape
    m = db * RP
    qn, nsakv, winf, _, slcb, winb, qf, kvf, kvfb, small = _proj(x, g_pre, w_proj, brow, m)
    kcs = _cmp_sample(page_table, cn, sc["pet"], sc["wcd"], layer)
    if n_pages < 64:
        kcs = jnp.pad(kcs.reshape(db, 2, G_NSA, n_pages, LANE), ((0, 0),) * 3 + ((0, 64 - n_pages), (0, 0)))
        kcs = kcs.reshape(db, 2, LANE, LANE)
    o_n = _nsa_sample(page_table, qn.reshape(db, RP, W_Q), small.reshape(db, RP, LANE), kcs, cn, wst,
                      slcb.reshape(db, RP, 256), winb.reshape(db, RP, 256), sc["eb"], layer, past, dseq)
    c_rows = _logf_sample(page_table, cl, sc["utri"], sc["slm"], layer)
    cflat = c_rows.reshape(db, n_pages, H_FOX // 2, 2, LANE).transpose(0, 2, 3, 1, 4)
    cflat = cflat.reshape(db, H_FOX // 2, 2, n_pages * LANE)
    clast = jnp.pad(cflat[:, :, :, -1].reshape(db, 1, H_FOX), ((0, 0), (0, 0), (LOGF_LANE, LANE - LOGF_LANE - H_FOX)))
    o_f = _fox_sample(page_table, qf.reshape(db, RP, W_QF), cflat, clast, small.reshape(db, RP, LANE),
                      kvfb.reshape(db, RP, W_KVF), sc["place"], cf, layer, dseq)
    y = _mixout(o_n.reshape(m, W_Q), o_f.reshape(m, W_QF), x, gg, w_o, g_post, m)
    return y, nsakv, winf, kvf, small


def _tiles(seq):
    return dict(tm=min(512, seq), nsa_tq=256, nsa_tk=min(512, seq), tc=min(256, seq), fox_tq=min(2048, seq),
                fox_tk=min(512, seq), fox_chunk=256, nsa_chunk=256, ffn_tm=min(512, seq))


def kernel(x_prompt, x_sample, cache_nsa_kv, state_nsa_win, cache_fox_kv, cache_fox_logf, state_conv, page_table,
           g_attn_pre, g_attn_post, g_ffn_pre, g_ffn_post, w_mix_in, b_gate, b_forget, cmp_pe, w_cmp_k, w_cmp_v,
           g_group, w_mix_out, w_ffn_in, conv_w, conv_b, w_ffn_out):
    depth = w_mix_in.shape[0]
    b, seq, d = x_prompt.shape
    db, dseq, _ = x_sample.shape
    dff = conv_b.shape[1]
    wb = min(NSA_WINDOW, seq)
    tiles = _tiles(seq)
    tf = dff // 2
    ltri, place = _fox_consts(tiles["tc"])
    consts = dict(cpos=_pos_aug(jnp.arange(LANE) * NSA_BLOCK + (NSA_BLOCK - 1)), kpos=_pos_aug(jnp.arange(seq)),
                  ebt=_expand_t(seq, LANE), ltri=ltri, place=place)

    n_pages, page = page_table.shape[1], cache_nsa_kv.shape[2]
    past = n_pages * page
    wbs = state_nsa_win.shape[2]
    assert page == LANE and past % NSA_BLOCK == 0 and dseq <= min(RP, NSA_BLOCK) and n_pages <= 64
    cn = jnp.transpose(cache_nsa_kv, (0, 1, 3, 4, 5, 2)).reshape(depth, -1, 4 * G_NSA * HEAD_DIM, page)
    cf = jnp.transpose(cache_fox_kv, (0, 1, 3, 4, 5, 2)).reshape(depth, -1, W_KVF, page)
    cl = jnp.transpose(cache_fox_logf, (0, 1, 3, 2))
    wst = jnp.transpose(state_nsa_win, (0, 1, 3, 4, 5, 2)).reshape(depth, db, 2 * G_NSA * HEAD_DIM, wbs)
    views = (cn, cf, cl, wst)
    utri, slm = _logf_consts(n_pages)
    eb_s = _expand_sample(n_pages)
    xs = jnp.pad(x_sample, ((0, 0), (0, RP - dseq), (0, 0))).reshape(db * RP, d)
    s_nsa, s_win, s_fox, s_logf, s_conv = [], [], [], [], []

    xp = x_prompt.reshape(b * seq, d)
    p_nsa, p_win, p_fox, p_logf, p_conv = [], [], [], [], []
    for l in range(depth):
        w_proj, brow, w_big, pe_rows, gg, w_o = _prep_layer_weights(
            w_mix_in[l], b_gate[l], b_forget[l], cmp_pe[l], w_cmp_k[l], w_cmp_v[l], g_group[l], w_mix_out[l])
        lw = (w_proj, brow, w_big, pe_rows, gg, w_o, g_attn_pre[l].reshape(1, d), g_attn_post[l].reshape(1, d))
        wa = w_ffn_in[l, :, :dff].astype(BF16)
        wbb = w_ffn_in[l, :, dff:].astype(BF16)
        wo = w_ffn_out[l].astype(BF16)
        ffn_w = (g_ffn_pre[l].reshape(1, d), wa, wbb, conv_w[l], conv_b[l].reshape(1, dff), wo,
                 g_ffn_post[l].reshape(1, d))

        xp, nsakv, winf, kvf, small = _mixer_prompt(xp, lw, consts, b, seq, tiles)
        xp, a_tail = _ffn(xp, None, *ffn_w, tm=tiles["ffn_tm"], tf=tf, seq=seq, tail=8)
        p_nsa.append(nsakv.reshape(b, 4, G_NSA, HEAD_DIM, seq).transpose(0, 4, 1, 2, 3))
        p_win.append(winf[:, :, seq - wb:].reshape(b, 2, G_NSA, HEAD_DIM, wb).transpose(0, 4, 1, 2, 3))
        p_fox.append(kvf.reshape(b, 2, H_FOX, HEAD_DIM, seq).transpose(0, 4, 1, 2, 3))
        p_logf.append(small.reshape(b, seq, LANE)[:, :, LOGF_LANE:LOGF_LANE + H_FOX])
        nbk = seq // tiles["ffn_tm"]
        p_conv.append(a_tail.reshape(b, nbk, 8, dff)[:, nbk - 1, 8 - (CONV_W - 1):])

        wcd, pet = _sample_consts(n_pages, w_cmp_k[l], w_cmp_v[l], cmp_pe[l])
        sc = dict(wcd=wcd, pet=pet, eb=eb_s, utri=utri, slm=slm, place=place)
        xs, nsakv, winf, kvf, small = _mixer_sample(xs, lw, sc, views, page_table, l, past, dseq)
        st = jnp.zeros((db + 1, RP, dff), F32).at[:db, RP - (CONV_W - 1):].set(state_conv[l])
        xs, a_full = _ffn(xs, st.reshape((db + 1) * RP, dff), *ffn_w, tm=db * RP, tf=tf, seq=RP, tail=db * RP)
        s_nsa.append(nsakv.reshape(db, RP, 4, G_NSA, HEAD_DIM)[:, :dseq])
        new_win = winf.reshape(db, RP, 2, G_NSA, HEAD_DIM)[:, :dseq]
        s_win.append(jnp.concatenate([state_nsa_win[l], new_win], axis=1)[:, dseq:])
        s_fox.append(kvf.reshape(db, RP, 2, H_FOX, HEAD_DIM)[:, :dseq])
        s_logf.append(small.reshape(db, RP, LANE)[:, :dseq, LOGF_LANE:LOGF_LANE + H_FOX])
        a_rows = jnp.concatenate([state_conv[l], a_full.reshape(db, RP, dff)[:, :dseq]], axis=1)
        s_conv.append(a_rows[:, dseq:])

    return (xp.reshape(b, seq, d), xs.reshape(db, RP, d)[:, :dseq],
            jnp.stack(p_nsa), jnp.stack(s_nsa),
            jnp.stack(p_win), jnp.stack(s_win),
            jnp.stack(p_fox), jnp.stack(s_fox),
            jnp.stack(p_logf), jnp.stack(s_logf),
            jnp.stack(p_conv), jnp.stack(s_conv))
```
